```python
import math
import jax
import jax.numpy as jnp
from jax import lax
import numpy as np


D_MODEL = 2048
BATCH = 4
SEQ = 4096
DEPTH = 4

CHUNK = 64
N_MEM = 256
N_EVEN = (DEPTH + 1) // 2
N_ODD = DEPTH // 2
EPS = 1e-6
A_HEADS = 8
A_HEAD_DIM = 128
A_WIDTH = A_HEADS * A_HEAD_DIM
A_CONV = 4
B_HEADS = 8
B_QK_DIM = 64
B_QK_WIDTH = B_HEADS * 2 * B_QK_DIM
B_V_DIM = 2 * B_QK_DIM
B_WIDTH = B_HEADS * B_V_DIM
Q_BLOCK = 128
ROPE_THETA = 10000.0
C_WIDTH = D_MODEL
C_CONV = 3
M_HEADS = 4
M_HEAD_DIM = 128
M_WIDTH = M_HEADS * M_HEAD_DIM
AB_SIZES = (3 * A_WIDTH, A_WIDTH, A_HEADS, A_HEADS, B_QK_WIDTH, B_QK_WIDTH, B_WIDTH, B_WIDTH)
AB_IN = 4 * A_WIDTH + 2 * A_HEADS + 2 * B_QK_WIDTH + 2 * B_WIDTH
AB_OUT = A_WIDTH + B_WIDTH

kernel_name = 'hybrid_gdn_diffattn_shortconv_trunk'


def rms_norm(x, gain):
    xf = x.astype(jnp.float32)
    y = xf * lax.rsqrt(jnp.mean(xf * xf, axis=-1, keepdims=True) + EPS)
    return (y * gain.astype(jnp.float32)).astype(x.dtype)


def l2_normalize(x):
    xf = x.astype(jnp.float32)
    return (xf * lax.rsqrt(jnp.sum(xf * xf, axis=-1, keepdims=True) + EPS)).astype(x.dtype)


def split_sizes(t, sizes):
    return jnp.split(t, np.cumsum(sizes)[:-1].tolist(), axis=-1)


def causal_depthwise_conv(u, w):
    k = w.shape[0]
    return lax.conv_general_dilated(u, w[:, None, :].astype(u.dtype), window_strides=(1,),
                                    padding=[(k - 1, 0)], dimension_numbers=('NWC', 'WIO', 'NWC'),
                                    feature_group_count=u.shape[-1])


def rope_tables(seq, dim):
    inv_freq = ROPE_THETA ** (-jnp.arange(0, dim, 2, dtype=jnp.float32) / dim)
    ang = jnp.arange(seq, dtype=jnp.float32)[:, None] * inv_freq[None, :]
    return jnp.cos(ang), jnp.sin(ang)


def apply_rope(t, cos, sin):
    tf = t.astype(jnp.float32)
    t1, t2 = jnp.split(tf, 2, axis=-1)
    c = cos[None, :, None, None, :]
    s = sin[None, :, None, None, :]
    return jnp.concatenate([t1 * c - t2 * s, t2 * c + t1 * s], axis=-1).astype(t.dtype)


def lambda_init(layer):
    return 0.8 - 0.6 * math.exp(-0.3 * layer)


def gated_delta_rule(q, k, v, g, beta):
    bsz, seq, heads, dk = q.shape
    dv = v.shape[-1]
    n = seq // CHUNK
    f32 = jnp.float32

    def chunks(t):
        t = t.astype(f32).reshape((bsz, n, CHUNK, heads) + t.shape[3:])
        return jnp.moveaxis(t, 3, 1)

    q = chunks(q) * (dk ** -0.5)
    k = chunks(k)
    v = chunks(v)
    g = chunks(g)
    beta = chunks(beta)
    decay = jnp.cumsum(g, axis=-1)
    tril = jnp.tril(jnp.ones((CHUNK, CHUNK), dtype=bool))
    strict = jnp.tril(jnp.ones((CHUNK, CHUNK), dtype=bool), -1)
    diff = decay[..., :, None] - decay[..., None, :]
    gamma = jnp.where(tril, jnp.exp(jnp.where(tril, diff, 0.0)), 0.0)
    k_beta = k * beta[..., None]
    n_mat = jnp.where(strict, jnp.einsum('bhnid,bhnjd->bhnij', k_beta, k) * gamma, 0.0)
    eye = jnp.eye(CHUNK, dtype=f32)
    rhs = jnp.concatenate([v * beta[..., None], k_beta * jnp.exp(decay)[..., None]], axis=-1)
    sol = lax.linalg.triangular_solve(eye + n_mat, rhs, left_side=True, lower=True, unit_diagonal=True)
    u, w = sol[..., :dv], sol[..., dv:]
    qk = jnp.where(tril, jnp.einsum('bhnid,bhnjd->bhnij', q, k) * gamma, 0.0)
    q_dec = q * jnp.exp(decay)[..., None]
    k_dec = k * jnp.exp(decay[..., -1:] - decay)[..., None]
    chunk_decay = jnp.exp(decay[..., -1])

    def step(state, xs):
        qk_i, qd_i, kd_i, u_i, w_i, cd_i = xs
        v_new = u_i - jnp.einsum('bhcd,bhde->bhce', w_i, state)
        o = jnp.einsum('bhcd,bhde->bhce', qd_i, state) + jnp.einsum('bhij,bhje->bhie', qk_i, v_new)
        state = state * cd_i[..., None, None] + jnp.einsum('bhcd,bhce->bhde', kd_i, v_new)
        return state, o

    xs = tuple(jnp.moveaxis(t, 2, 0) for t in (qk, q_dec, k_dec, u, w, chunk_decay))
    state0 = jnp.zeros((bsz, heads, dk, dv), f32)
    _, o = lax.scan(step, state0, xs)
    return jnp.transpose(o, (1, 0, 3, 2, 4)).reshape(bsz, seq, heads, dv)


def differential_attention(q, k, v, lam):
    bsz, seq = q.shape[:2]
    nqb = seq // Q_BLOCK
    scale = B_QK_DIM ** -0.5
    qb = jnp.transpose(q.reshape(bsz, nqb, Q_BLOCK, B_HEADS, 2, B_QK_DIM), (1, 0, 3, 4, 2, 5))
    kt = jnp.transpose(k, (0, 2, 3, 1, 4))
    vt = jnp.transpose(v, (0, 2, 1, 3)).astype(jnp.float32)
    key_chunk = jnp.arange(seq) // CHUNK

    def one_block(args):
        blk, q_blk = args
        s = jnp.einsum('bhmqd,bhmkd->bhmqk', q_blk, kt, preferred_element_type=jnp.float32) * scale
        q_chunk = (blk * Q_BLOCK + jnp.arange(Q_BLOCK)) // CHUNK
        allowed = key_chunk[None, :] <= q_chunk[:, None]
        p = jax.nn.softmax(jnp.where(allowed, s, -jnp.inf), axis=-1)
        a = p[:, :, 0] - lam * p[:, :, 1]
        return jnp.einsum('bhqk,bhkd->bhqd', a, vt)

    o = lax.map(one_block, (jnp.arange(nqb), qb))
    return jnp.transpose(o, (1, 0, 3, 2, 4)).reshape(bsz, seq, B_HEADS, B_V_DIM)


def ab_mixer(h, w_in, conv_w, a_log, dt_bias, a_norm, lam_params, b_norm, w_out, lam_init, cos, sin):
    bsz, seq, _ = h.shape
    qkv_a, z_a, b_a, g_a, q_b, k_b, v_b, z_b = split_sizes(h @ w_in, AB_SIZES)
    qkv_a = jax.nn.silu(causal_depthwise_conv(qkv_a, conv_w))
    q_a, k_a, v_a = jnp.split(qkv_a, 3, axis=-1)
    a_heads = lambda t: t.reshape(bsz, seq, A_HEADS, A_HEAD_DIM)
    q_a = l2_normalize(a_heads(q_a))
    k_a = l2_normalize(a_heads(k_a))
    beta = jax.nn.sigmoid(b_a.astype(jnp.float32))
    log_decay = -jnp.exp(a_log.astype(jnp.float32)) * jax.nn.softplus(g_a.astype(jnp.float32) + dt_bias.astype(jnp.float32))
    o_a = gated_delta_rule(q_a, k_a, a_heads(v_a), log_decay, beta).astype(h.dtype)
    y_a = rms_norm(o_a, a_norm) * jax.nn.silu(a_heads(z_a))
    q_b = apply_rope(q_b.reshape(bsz, seq, B_HEADS, 2, B_QK_DIM), cos, sin)
    k_b = apply_rope(k_b.reshape(bsz, seq, B_HEADS, 2, B_QK_DIM), cos, sin)
    lp = lam_params.astype(jnp.float32)
    lam = jnp.exp(jnp.sum(lp[0] * lp[1])) - jnp.exp(jnp.sum(lp[2] * lp[3])) + lam_init
    o_b = differential_attention(q_b, k_b, v_b.reshape(bsz, seq, B_HEADS, B_V_DIM), lam).astype(h.dtype)
    y_b = rms_norm(o_b, b_norm) * (1.0 - lam_init) * jax.nn.silu(z_b.reshape(bsz, seq, B_HEADS, B_V_DIM))
    y = jnp.concatenate([y_a.reshape(bsz, seq, A_WIDTH), y_b.reshape(bsz, seq, B_WIDTH)], axis=-1)
    return y @ w_out


def conv_mixer(h, w_in, conv_w, w_out):
    u, b_gate, c_gate, z = jnp.split(h @ w_in, 4, axis=-1)
    y = b_gate * causal_depthwise_conv(c_gate * u, conv_w)
    return (y * jax.nn.silu(z)) @ w_out


def memory_cross_attention(h, mem_h, w_q, w_kv, w_o):
    bsz, seq, _ = h.shape
    n_mem = mem_h.shape[1]
    q, z = jnp.split(h @ w_q, 2, axis=-1)
    k, v = jnp.split(mem_h @ w_kv, 2, axis=-1)
    q = q.reshape(bsz, seq, M_HEADS, M_HEAD_DIM)
    k = k.reshape(bsz, n_mem, M_HEADS, M_HEAD_DIM)
    v = v.reshape(bsz, n_mem, M_HEADS, M_HEAD_DIM)
    s = jnp.einsum('bshd,bmhd->bhsm', q, k, preferred_element_type=jnp.float32) * (M_HEAD_DIM ** -0.5)
    p = jax.nn.softmax(s, axis=-1)
    o = jnp.einsum('bhsm,bmhd->bshd', p.astype(v.dtype), v).reshape(bsz, seq, M_WIDTH)
    return (o * jax.nn.silu(z)) @ w_o


def setup_inputs(seed: int = 0) -> dict:
    key = jax.random.key(seed)
    ks = jax.random.split(key, 22)
    f32 = jnp.float32

    def normal(k, shape, scale):
        return jax.random.normal(k, shape, f32) * scale

    def gain(k, shape):
        return 1.0 + normal(k, shape, 0.05)

    dt = jnp.exp(jax.random.uniform(ks[10], (N_EVEN, A_HEADS), f32, math.log(1e-3), math.log(1e-1)))
    return {
        'x': normal(ks[0], (BATCH, SEQ, D_MODEL), 1.0),
        'mem': normal(ks[1], (BATCH, N_MEM, D_MODEL), 1.0),
        'norm_pre': gain(ks[2], (DEPTH, D_MODEL)),
        'norm_post': gain(ks[3], (DEPTH, D_MODEL)),
        'xattn_norm_pre': gain(ks[4], (DEPTH, D_MODEL)),
        'xattn_norm_post': gain(ks[5], (DEPTH, D_MODEL)),
        'mem_norm': gain(ks[6], (DEPTH, D_MODEL)),
        'ab_w_in': normal(ks[7], (N_EVEN, D_MODEL, AB_IN), D_MODEL ** -0.5),
        'a_conv': normal(ks[8], (N_EVEN, A_CONV, 3 * A_WIDTH), A_CONV ** -0.5),
        'a_A_log': jnp.log(jax.random.uniform(ks[9], (N_EVEN, A_HEADS), f32, 1.0, 16.0)),
        'a_dt_bias': jnp.log(jnp.expm1(dt)),
        'a_out_norm': gain(ks[11], (N_EVEN, A_HEAD_DIM)),
        'b_lambda': normal(ks[12], (N_EVEN, 4, B_QK_DIM), 0.1),
        'b_out_norm': gain(ks[13], (N_EVEN, B_V_DIM)),
        'ab_w_out': normal(ks[14], (N_EVEN, AB_OUT, D_MODEL), AB_OUT ** -0.5),
        'c_w_in': normal(ks[15], (N_ODD, D_MODEL, 4 * C_WIDTH), D_MODEL ** -0.5),
        'c_conv': normal(ks[16], (N_ODD, C_CONV, C_WIDTH), C_CONV ** -0.5),
        'c_w_out': normal(ks[17], (N_ODD, C_WIDTH, D_MODEL), C_WIDTH ** -0.5),
        'm_w_q': normal(ks[18], (DEPTH, D_MODEL, 2 * M_WIDTH), D_MODEL ** -0.5),
        'm_w_kv': normal(ks[19], (DEPTH, D_MODEL, 2 * M_WIDTH), D_MODEL ** -0.5),
        'm_w_o': normal(ks[20], (DEPTH, M_WIDTH, D_MODEL), M_WIDTH ** -0.5),
    }


def reference(x, mem, norm_pre, norm_post, xattn_norm_pre, xattn_norm_post, mem_norm,
              ab_w_in, a_conv, a_A_log, a_dt_bias, a_out_norm, b_lambda, b_out_norm, ab_w_out,
              c_w_in, c_conv, c_w_out, m_w_q, m_w_kv, m_w_o):
    cos, sin = rope_tables(x.shape[1], B_QK_DIM)
    for layer in range(DEPTH):
        h = rms_norm(x, norm_pre[layer])
        if layer % 2 == 0:
            e = layer // 2
            y = ab_mixer(h, ab_w_in[e], a_conv[e], a_A_log[e], a_dt_bias[e], a_out_norm[e],
                         b_lambda[e], b_out_norm[e], ab_w_out[e], lambda_init(layer), cos, sin)
        else:
            o = layer // 2
            y = conv_mixer(h, c_w_in[o], c_conv[o], c_w_out[o])
        x = x + rms_norm(y, norm_post[layer])
        h = rms_norm(x, xattn_norm_pre[layer])
        m = rms_norm(mem, mem_norm[layer])
        y = memory_cross_attention(h, m, m_w_q[layer], m_w_kv[layer], m_w_o[layer])
        x = x + rms_norm(y, xattn_norm_post[layer])
    return x
```

```python
import functools
import math

import jax
import jax.numpy as jnp
from jax import lax
from jax.experimental import pallas as pl
from jax.experimental.pallas import tpu as pltpu

F32 = jnp.float32
BF16 = jnp.bfloat16

EPS = 1e-6
CHUNK = 64
PACK = 256
A_HEADS = 8
A_HEAD_DIM = 128
A_WIDTH = A_HEADS * A_HEAD_DIM
A_CONV = 4
B_HEADS = 8
B_QK_DIM = 64
B_V_DIM = 128
B_WIDTH = B_HEADS * B_V_DIM
ROPE_THETA = 10000.0
C_CONV = 3
M_HEADS = 4
M_HEAD_DIM = 128
M_WIDTH = M_HEADS * M_HEAD_DIM
LANES = 128
SUBLANES = 8
NEG_BIG = -1e30
VMEM_LIMIT = 56 * 1024 * 1024


def _cparams(sem):
    return pltpu.CompilerParams(dimension_semantics=sem, vmem_limit_bytes=VMEM_LIMIT)


def _mm(a, b):
    return jnp.dot(a.astype(BF16), b.astype(BF16), preferred_element_type=F32)


def _mm_nt(a, b):
    return lax.dot_general(a.astype(BF16), b.astype(BF16), (((1,), (1,)), ((), ())),
                           preferred_element_type=F32)


def _mm_tn(a, b):
    return lax.dot_general(a.astype(BF16), b.astype(BF16), (((0,), (0,)), ((), ())),
                           preferred_element_type=F32)


def _rms(x, gain):
    return x * lax.rsqrt(jnp.mean(x * x, axis=-1, keepdims=True) + EPS) * gain


def _sigmoid(x):
    return 1.0 / (1.0 + jnp.exp(-x))


def _silu(x):
    return x * _sigmoid(x)


def _softplus(x):
    return jnp.maximum(x, 0.0) + jnp.log1p(jnp.exp(-jnp.abs(x)))


def _shifted_rows(tail, x, shift):
    ext = jnp.concatenate([tail, x], axis=0)
    return pltpu.roll(ext, shift, axis=0)[SUBLANES:]


def _norm_matmul_kernel(x_ref, g_ref, w_ref, o_ref, h_ref):
    @pl.when(pl.program_id(1) == 0)
    def _():
        h_ref[...] = _rms(x_ref[...], g_ref[...]).astype(BF16)

    o_ref[...] = jnp.dot(h_ref[...], w_ref[...], preferred_element_type=F32).astype(o_ref.dtype)


def _norm_matmul_gates_kernel(x_ref, g_ref, w_ref, wg_ref, o_ref, og_ref, h_ref):
    @pl.when(pl.program_id(1) == 0)
    def _():
        h_ref[...] = _rms(x_ref[...], g_ref[...]).astype(BF16)
        og_ref[...] = jnp.dot(h_ref[...], wg_ref[...], preferred_element_type=F32)

    o_ref[...] = jnp.dot(h_ref[...], w_ref[...], preferred_element_type=F32).astype(o_ref.dtype)


def norm_matmul(x, gain, w, *, tm, tn, w_gates=None):
    t, d = x.shape
    n = w.shape[1]
    tm = min(tm, t)
    grid = (t // tm, n // tn)
    x_spec = pl.BlockSpec((tm, d), lambda i, j: (i, 0))
    g_spec = pl.BlockSpec((1, d), lambda i, j: (0, 0))
    w_spec = pl.BlockSpec((d, tn), lambda i, j: (0, j))
    o_spec = pl.BlockSpec((tm, tn), lambda i, j: (i, j))
    scratch = [pltpu.VMEM((tm, d), BF16)]
    if w_gates is None:
        return pl.pallas_call(
            _norm_matmul_kernel, grid=grid,
            in_specs=[x_spec, g_spec, w_spec], out_specs=o_spec,
            out_shape=jax.ShapeDtypeStruct((t, n), BF16),
            scratch_shapes=scratch, compiler_params=_cparams(("parallel", "arbitrary")),
            name="norm_matmul",
        )(x, gain.reshape(1, d), w)
    return pl.pallas_call(
        _norm_matmul_gates_kernel, grid=grid,
        in_specs=[x_spec, g_spec, w_spec, pl.BlockSpec((d, LANES), lambda i, j: (0, 0))],
        out_specs=[o_spec, pl.BlockSpec((tm, LANES), lambda i, j: (i, 0))],
        out_shape=[jax.ShapeDtypeStruct((t, n), BF16), jax.ShapeDtypeStruct((t, LANES), F32)],
        scratch_shapes=scratch, compiler_params=_cparams(("parallel", "arbitrary")),
        name="norm_matmul_gates",
    )(x, gain.reshape(1, d), w, w_gates)


def _rope_kernel(q_ref, k_ref, cos_ref, sin_ref, qo_ref, ko_ref):
    cos = cos_ref[...]
    sin = sin_ref[...]
    lane = lax.broadcasted_iota(jnp.int32, cos.shape, 1)
    first_half = (lane & 32) == 0

    def rot(ref, out_ref, scale):
        for h in range(B_HEADS):
            sl = slice(h * LANES, (h + 1) * LANES)
            t = ref[:, sl].astype(F32)
            partner = jnp.where(first_half, pltpu.roll(t, LANES - 32, axis=1), pltpu.roll(t, 32, axis=1))
            out_ref[:, sl] = ((t * cos + partner * sin) * scale).astype(out_ref.dtype)

    rot(q_ref, qo_ref, B_QK_DIM ** -0.5)
    rot(k_ref, ko_ref, 1.0)


def rope_qk(proj, cos_t, sin_t, *, seq, tm):
    t = proj.shape[0]
    tm = min(tm, seq)
    spb = seq // tm
    qcol = 4 * A_WIDTH // B_WIDTH
    return pl.pallas_call(
        _rope_kernel, grid=(t // tm,),
        in_specs=[pl.BlockSpec((tm, B_WIDTH), lambda i: (i, qcol)),
                  pl.BlockSpec((tm, B_WIDTH), lambda i: (i, qcol + 1)),
                  pl.BlockSpec((tm, LANES), lambda i: (i % spb, 0)),
                  pl.BlockSpec((tm, LANES), lambda i: (i % spb, 0))],
        out_specs=[pl.BlockSpec((tm, B_WIDTH), lambda i: (i, 0)),
                   pl.BlockSpec((tm, B_WIDTH), lambda i: (i, 0))],
        out_shape=[jax.ShapeDtypeStruct((t, B_WIDTH), BF16)] * 2,
        compiler_params=_cparams(("parallel",)), name="rope_qk",
    )(proj, proj, cos_t, sin_t)


def _gdn_kernel(q_ref, k_ref, v_ref, z_ref, bg_ref, cwq_ref, cwk_ref, cwv_ref,
                arow_ref, dtrow_ref, anorm_ref, o_ref,
                state_ref, tq_ref, tk_ref, tv_ref, *, ts, hb):
    hg = pl.program_id(1)

    @pl.when(pl.program_id(2) == 0)
    def _():
        state_ref[...] = jnp.zeros_like(state_ref)
        tq_ref[...] = jnp.zeros_like(tq_ref)
        tk_ref[...] = jnp.zeros_like(tk_ref)
        tv_ref[...] = jnp.zeros_like(tv_ref)

    def conv_silu(x_ref, w_ref, tail_ref):
        x = x_ref[...].astype(F32)
        tail = tail_ref[...]
        w = w_ref[...]
        acc = x * w[A_CONV - 1:A_CONV]
        for j in range(A_CONV - 1):
            acc = acc + _shifted_rows(tail, x, A_CONV - 1 - j) * w[j:j + 1]
        tail_ref[...] = x[ts - SUBLANES:]
        return _silu(acc)

    q_all = conv_silu(q_ref, cwq_ref, tq_ref)
    k_all = conv_silu(k_ref, cwk_ref, tk_ref)
    v_all = conv_silu(v_ref, cwv_ref, tv_ref)

    bg = bg_ref[...]
    beta_all = _sigmoid(bg)
    dec = -jnp.exp(arow_ref[...]) * _softplus(bg + dtrow_ref[...])
    row_in_chunk = lax.broadcasted_iota(jnp.int32, (ts, 1), 0) & (CHUNK - 1)
    step = 1
    while step < CHUNK:
        dec = dec + jnp.where(row_in_chunk >= step, pltpu.roll(dec, step, axis=0), 0.0)
        step *= 2
    lane = lax.broadcasted_iota(jnp.int32, (ts, LANES), 1)

    ii = lax.broadcasted_iota(jnp.int32, (PACK, PACK), 0)
    jj = lax.broadcasted_iota(jnp.int32, (PACK, PACK), 1)
    same64 = (ii >> 6) == (jj >> 6)
    tril = same64 & (ii >= jj)
    strict = same64 & (ii > jj)
    same16 = (ii >> 4) == (jj >> 4)
    same32 = (ii >> 5) == (jj >> 5)
    eye = jnp.where(ii == jj, 1.0, 0.0)

    for j in range(hb):
        head = hg * hb + j
        sl = slice(j * A_HEAD_DIM, (j + 1) * A_HEAD_DIM)
        qh = q_all[:, sl]
        kh = k_all[:, sl]
        vh = v_all[:, sl]
        qh = qh * (lax.rsqrt(jnp.sum(qh * qh, axis=-1, keepdims=True) + EPS) * (A_HEAD_DIM ** -0.5))
        kh = kh * lax.rsqrt(jnp.sum(kh * kh, axis=-1, keepdims=True) + EPS)
        beta = jnp.sum(jnp.where(lane == head, beta_all, 0.0), axis=-1, keepdims=True)
        d = jnp.sum(jnp.where(lane == head + A_HEADS, dec, 0.0), axis=-1, keepdims=True)
        exp_d = jnp.exp(d)
        kb = kh * beta
        rhs = jnp.concatenate([vh * beta, kb * exp_d], axis=-1)
        qd = qh * exp_d

        state = state_ref[j]
        outs = []
        for p in range(ts // PACK):
            rows = slice(p * PACK, (p + 1) * PACK)
            dp = d[rows]
            drow = jnp.broadcast_to(dp, (PACK, LANES)).T[0:1, :]
            gamma = jnp.where(tril, jnp.exp(jnp.where(tril, dp - drow, 0.0)), 0.0)
            nmat = jnp.where(strict, _mm_nt(kb[rows], kh[rows]) * gamma, 0.0)
            qk = _mm_nt(qh[rows], kh[rows]) * gamma
            p1 = jnp.where(same16, -nmat, 0.0)
            inv = eye + p1
            p2 = _mm(p1, p1)
            inv = inv + _mm(inv, p2)
            p4 = _mm(p2, p2)
            inv = inv + _mm(inv, p4)
            p8 = _mm(p4, p4)
            inv = inv + _mm(inv, p8)
            e1 = jnp.where(same32 & jnp.logical_not(same16), nmat, 0.0)
            inv = inv - _mm(inv, _mm(e1, inv))
            e2 = jnp.where(same32, 0.0, nmat)
            inv = inv - _mm(inv, _mm(e2, inv))
            sol = _mm(inv, rhs[rows])
            u = sol[:, :A_HEAD_DIM]
            w = sol[:, A_HEAD_DIM:]

            v_new = []
            o_inter = []
            for c in range(PACK // CHUNK):
                cr = slice(c * CHUNK, (c + 1) * CHUNK)
                gr = slice(p * PACK + c * CHUNK, p * PACK + (c + 1) * CHUNK)
                d_c = d[gr]
                d_last = d_c[CHUNK - 1:CHUNK]
                vn = u[cr] - _mm(w[cr], state)
                o_inter.append(_mm(qd[gr], state))
                kd = kh[gr] * jnp.exp(d_last - d_c)
                state = state * jnp.exp(d_last) + _mm_tn(kd, vn)
                v_new.append(vn)
            outs.append(jnp.concatenate(o_inter, axis=0) + _mm(qk, jnp.concatenate(v_new, axis=0)))
        state_ref[j] = state
        o = jnp.concatenate(outs, axis=0) if len(outs) > 1 else outs[0]
        y = _rms(o, anorm_ref[...]) * _silu(z_ref[:, sl].astype(F32))
        o_ref[:, sl] = y.astype(o_ref.dtype)


def gdn_heads(proj, gates, conv_w, a_row, dt_row, a_norm, *, batch, seq, ts, hb):
    t = proj.shape[0]
    ts = min(ts, seq)
    nst = seq // ts
    ngrp = A_HEADS // hb
    wblk = hb * A_HEAD_DIM

    def col(off):
        return pl.BlockSpec((ts, wblk), lambda b, g, s: (b * nst + s, off * ngrp + g))

    def cw(off):
        return pl.BlockSpec((A_CONV, wblk), lambda b, g, s: (0, off * ngrp + g))

    row = pl.BlockSpec((1, LANES), lambda b, g, s: (0, 0))
    return pl.pallas_call(
        functools.partial(_gdn_kernel, ts=ts, hb=hb),
        grid=(batch, ngrp, nst),
        in_specs=[col(0), col(1), col(2), col(3),
                  pl.BlockSpec((ts, LANES), lambda b, g, s: (b * nst + s, 0)),
                  cw(0), cw(1), cw(2), row, row, row],
        out_specs=pl.BlockSpec((ts, wblk), lambda b, g, s: (b * nst + s, g)),
        out_shape=jax.ShapeDtypeStruct((t, A_WIDTH), BF16),
        scratch_shapes=[pltpu.VMEM((hb, A_HEAD_DIM, A_HEAD_DIM), F32),
                        pltpu.VMEM((SUBLANES, wblk), F32),
                        pltpu.VMEM((SUBLANES, wblk), F32),
                        pltpu.VMEM((SUBLANES, wblk), F32)],
        compiler_params=_cparams(("parallel", "parallel", "arbitrary")),
        name="gdn_heads",
    )(proj, proj, proj, proj, gates, conv_w, conv_w, conv_w, a_row, dt_row, a_norm)


def _dattn_kernel(q_ref, k_ref, v_ref, z_ref, lamp_ref, bnorm_ref, o_ref,
                  qs_ref, m_ref, l_ref, acc_ref, *, tq, lam_init):
    qi = pl.program_id(2)
    q = q_ref[...]
    lane = lax.broadcasted_iota(jnp.int32, q.shape, 1)
    zero = jnp.zeros_like(q)
    qs_ref[0:tq, :] = jnp.where(lane < B_QK_DIM, q, zero)
    qs_ref[tq:, :] = jnp.where(lane < B_QK_DIM, zero, q)
    m_ref[...] = jnp.full_like(m_ref, NEG_BIG)
    l_ref[...] = jnp.zeros_like(l_ref)
    acc_ref[...] = jnp.zeros_like(acc_ref)

    def update(s, v_blk):
        m_prev = m_ref[...]
        m_new = jnp.maximum(m_prev, jnp.max(s, axis=-1, keepdims=True))
        alpha = jnp.exp(m_prev - m_new)
        p = jnp.exp(s - m_new)
        l_ref[...] = alpha * l_ref[...] + jnp.sum(p, axis=-1, keepdims=True)
        acc_ref[...] = alpha * acc_ref[...] + _mm(p, v_blk)
        m_ref[...] = m_new

    def body(ki, carry):
        start = pl.multiple_of(ki * tq, tq)
        update(_mm_nt(qs_ref[...], k_ref[pl.ds(start, tq), :]), v_ref[pl.ds(start, tq), :])
        return carry

    lax.fori_loop(0, qi, body, 0)

    start = pl.multiple_of(qi * tq, tq)
    s = _mm_nt(qs_ref[...], k_ref[pl.ds(start, tq), :])
    row = lax.broadcasted_iota(jnp.int32, s.shape, 0) & (tq - 1)
    colk = lax.broadcasted_iota(jnp.int32, s.shape, 1)
    s = jnp.where((colk >> 6) <= (row >> 6), s, NEG_BIG)
    update(s, v_ref[pl.ds(start, tq), :])

    lp = lamp_ref[...]
    lam = (jnp.exp(jnp.sum(lp[0:1] * lp[1:2], axis=-1, keepdims=True))
           - jnp.exp(jnp.sum(lp[2:3] * lp[3:4], axis=-1, keepdims=True)) + lam_init)
    o_all = acc_ref[...] / l_ref[...]
    o = o_all[:tq] - lam * o_all[tq:]
    y = _rms(o, bnorm_ref[...]) * (1.0 - lam_init) * _silu(z_ref[...].astype(F32))
    o_ref[...] = y.astype(o_ref.dtype)


def dattn_heads(q_rot, k_rot, proj, lam_params, b_norm, *, batch, seq, tq, lam_init):
    t = q_rot.shape[0]
    tq = min(tq, seq)
    nq = seq // tq
    vcol = (4 * A_WIDTH + 2 * B_WIDTH) // LANES
    zcol = (4 * A_WIDTH + 3 * B_WIDTH) // LANES
    return pl.pallas_call(
        functools.partial(_dattn_kernel, tq=tq, lam_init=lam_init),
        grid=(batch, B_HEADS, nq),
        in_specs=[pl.BlockSpec((tq, LANES), lambda b, h, i: (b * nq + i, h)),
                  pl.BlockSpec((seq, LANES), lambda b, h, i: (b, h)),
                  pl.BlockSpec((seq, LANES), lambda b, h, i: (b, vcol + h)),
                  pl.BlockSpec((tq, LANES), lambda b, h, i: (b * nq + i, zcol + h)),
                  pl.BlockSpec((4, B_QK_DIM), lambda b, h, i: (0, 0)),
                  pl.BlockSpec((1, LANES), lambda b, h, i: (0, 0))],
        out_specs=pl.BlockSpec((tq, LANES), lambda b, h, i: (b * nq + i, h)),
        out_shape=jax.ShapeDtypeStruct((t, B_WIDTH), BF16),
        scratch_shapes=[pltpu.VMEM((2 * tq, LANES), BF16),
                        pltpu.VMEM((2 * tq, 1), F32),
                        pltpu.VMEM((2 * tq, 1), F32),
                        pltpu.VMEM((2 * tq, LANES), F32)],
        compiler_params=_cparams(("parallel", "parallel", "arbitrary")),
        name="dattn_heads",
    )(q_rot, k_rot, proj, proj, lam_params, b_norm)


def _ab_out_kernel(ya_ref, yb_ref, wa_ref, wb_ref, g_ref, x_ref, o_ref):
    y = (jnp.dot(ya_ref[...], wa_ref[...], preferred_element_type=F32)
         + jnp.dot(yb_ref[...], wb_ref[...], preferred_element_type=F32))
    o_ref[...] = x_ref[...] + _rms(y, g_ref[...])


def ab_out_proj(y_a, y_b, w_a, w_b, gain, x, *, tm):
    t, d = x.shape
    tm = min(tm, t)
    return pl.pallas_call(
        _ab_out_kernel, grid=(t // tm,),
        in_specs=[pl.BlockSpec((tm, A_WIDTH), lambda i: (i, 0)),
                  pl.BlockSpec((tm, B_WIDTH), lambda i: (i, 0)),
                  pl.BlockSpec((A_WIDTH, d), lambda i: (0, 0)),
                  pl.BlockSpec((B_WIDTH, d), lambda i: (0, 0)),
                  pl.BlockSpec((1, d), lambda i: (0, 0)),
                  pl.BlockSpec((tm, d), lambda i: (i, 0))],
        out_specs=pl.BlockSpec((tm, d), lambda i: (i, 0)),
        out_shape=jax.ShapeDtypeStruct((t, d), F32),
        compiler_params=_cparams(("parallel",)), name="ab_out_proj",
    )(y_a, y_b, w_a, w_b, gain.reshape(1, d), x)


def _conv_out_kernel(u_ref, b_ref, c_ref, z_ref, cw_ref, w_ref, g_ref, x_ref, o_ref, tail_ref,
                     *, tm, tiles_per_seq):
    @pl.when(pl.program_id(0) % tiles_per_seq == 0)
    def _():
        tail_ref[...] = jnp.zeros_like(tail_ref)

    cu = c_ref[...].astype(F32) * u_ref[...].astype(F32)
    tail = tail_ref[...]
    cw = cw_ref[...]
    conv = cu * cw[C_CONV - 1:C_CONV]
    for j in range(C_CONV - 1):
        conv = conv + _shifted_rows(tail, cu, C_CONV - 1 - j) * cw[j:j + 1]
    tail_ref[...] = cu[tm - SUBLANES:]
    gated = b_ref[...].astype(F32) * conv * _silu(z_ref[...].astype(F32))
    y = jnp.dot(gated.astype(BF16), w_ref[...], preferred_element_type=F32)
    o_ref[...] = x_ref[...] + _rms(y, g_ref[...])


def conv_out_proj(proj, conv_w, w_out, gain, x, *, seq, tm):
    t, d = x.shape
    tm = min(tm, seq)

    def col(c):
        return pl.BlockSpec((tm, d), lambda i: (i, c))

    return pl.pallas_call(
        functools.partial(_conv_out_kernel, tm=tm, tiles_per_seq=seq // tm),
        grid=(t // tm,),
        in_specs=[col(0), col(1), col(2), col(3),
                  pl.BlockSpec((C_CONV, d), lambda i: (0, 0)),
                  pl.BlockSpec((d, d), lambda i: (0, 0)),
                  pl.BlockSpec((1, d), lambda i: (0, 0)),
                  pl.BlockSpec((tm, d), lambda i: (i, 0))],
        out_specs=pl.BlockSpec((tm, d), lambda i: (i, 0)),
        out_shape=jax.ShapeDtypeStruct((t, d), F32),
        scratch_shapes=[pltpu.VMEM((SUBLANES, d), F32)],
        compiler_params=_cparams(("arbitrary",)), name="conv_out_proj",
    )(proj, proj, proj, proj, conv_w, w_out, gain.reshape(1, d), x)


def _xattn_kernel(x_ref, gpre_ref, wq_ref, k_ref, v_ref, wo_ref, gpost_ref, o_ref):
    x = x_ref[...]
    h = _rms(x, gpre_ref[...]).astype(BF16)
    qz = jnp.dot(h, wq_ref[...], preferred_element_type=F32)
    q = qz[:, :M_WIDTH] * (M_HEAD_DIM ** -0.5)
    z = qz[:, M_WIDTH:]
    heads = []
    for hd in range(M_HEADS):
        sl = slice(hd * M_HEAD_DIM, (hd + 1) * M_HEAD_DIM)
        s = _mm_nt(q[:, sl], k_ref[:, sl])
        p = jnp.exp(s - jnp.max(s, axis=-1, keepdims=True))
        p = p / jnp.sum(p, axis=-1, keepdims=True)
        heads.append(_mm(p, v_ref[:, sl]))
    o = jnp.concatenate(heads, axis=-1) * _silu(z)
    y = jnp.dot(o.astype(BF16), wo_ref[...], preferred_element_type=F32)
    o_ref[...] = x + _rms(y, gpost_ref[...])


def xattn_sublayer(x, g_pre, w_q, kv, w_o, g_post, *, seq, n_mem, tm):
    t, d = x.shape
    tm = min(tm, seq)
    spb = seq // tm
    return pl.pallas_call(
        _xattn_kernel, grid=(t // tm,),
        in_specs=[pl.BlockSpec((tm, d), lambda i: (i, 0)),
                  pl.BlockSpec((1, d), lambda i: (0, 0)),
                  pl.BlockSpec((d, 2 * M_WIDTH), lambda i: (0, 0)),
                  pl.BlockSpec((n_mem, M_WIDTH), lambda i: (i // spb, 0)),
                  pl.BlockSpec((n_mem, M_WIDTH), lambda i: (i // spb, 1)),
                  pl.BlockSpec((M_WIDTH, d), lambda i: (0, 0)),
                  pl.BlockSpec((1, d), lambda i: (0, 0))],
        out_specs=pl.BlockSpec((tm, d), lambda i: (i, 0)),
        out_shape=jax.ShapeDtypeStruct((t, d), F32),
        compiler_params=_cparams(("parallel",)), name="xattn_sublayer",
    )(x, g_pre.reshape(1, d), w_q, kv, kv, w_o, g_post.reshape(1, d))


def _lambda_init(layer):
    return 0.8 - 0.6 * math.exp(-0.3 * layer)


def _rope_lane_tables(seq):
    half = B_QK_DIM // 2
    inv_freq = ROPE_THETA ** (-jnp.arange(0, B_QK_DIM, 2, dtype=F32) / B_QK_DIM)
    ang = jnp.arange(seq, dtype=F32)[:, None] * inv_freq[None, :]
    cos, sin = jnp.cos(ang), jnp.sin(ang)
    assert cos.shape == (seq, half)
    return jnp.tile(cos, (1, 4)), jnp.concatenate([-sin, sin, -sin, sin], axis=-1)


def _lane_row(values, offset):
    return jnp.zeros((1, LANES), F32).at[0, offset:offset + values.shape[0]].set(values.astype(F32))


def kernel(x, mem, norm_pre, norm_post, xattn_norm_pre, xattn_norm_post, mem_norm,
           ab_w_in, a_conv, a_A_log, a_dt_bias, a_out_norm, b_lambda, b_out_norm, ab_w_out,
           c_w_in, c_conv, c_w_out, m_w_q, m_w_kv, m_w_o):
    batch, seq, d = x.shape
    n_mem = mem.shape[1]
    depth = norm_pre.shape[0]
    cos_t, sin_t = _rope_lane_tables(seq)
    xf = x.reshape(batch * seq, d)
    memf = mem.reshape(batch * n_mem, d)
    gate_lo = 4 * A_WIDTH
    gate_hi = gate_lo + 2 * A_HEADS

    for layer in range(depth):
        if layer % 2 == 0:
            e = layer // 2
            w_in = ab_w_in[e]
            w_main = jnp.concatenate([w_in[:, :gate_lo], w_in[:, gate_hi:]], axis=1).astype(BF16)
            w_gates = jnp.pad(w_in[:, gate_lo:gate_hi], ((0, 0), (0, LANES - 2 * A_HEADS))).astype(BF16)
            proj, gates = norm_matmul(xf, norm_pre[layer], w_main, tm=1024, tn=1024, w_gates=w_gates)
            y_a = gdn_heads(proj, gates, a_conv[e], _lane_row(a_A_log[e], A_HEADS),
                            _lane_row(a_dt_bias[e], A_HEADS), a_out_norm[e].reshape(1, A_HEAD_DIM),
                            batch=batch, seq=seq, ts=512, hb=2)
            q_rot, k_rot = rope_qk(proj, cos_t, sin_t, seq=seq, tm=512)
            y_b = dattn_heads(q_rot, k_rot, proj, b_lambda[e], b_out_norm[e].reshape(1, B_V_DIM),
                              batch=batch, seq=seq, tq=256, lam_init=_lambda_init(layer))
            w_out = ab_w_out[e].astype(BF16)
            xf = ab_out_proj(y_a, y_b, w_out[:A_WIDTH], w_out[A_WIDTH:], norm_post[layer], xf, tm=512)
        else:
            o = layer // 2
            proj = norm_matmul(xf, norm_pre[layer], c_w_in[o].astype(BF16), tm=1024, tn=1024)
            xf = conv_out_proj(proj, c_conv[o], c_w_out[o].astype(BF16), norm_post[layer], xf,
                               seq=seq, tm=256)
        kv = norm_matmul(memf, mem_norm[layer], m_w_kv[layer].astype(BF16), tm=512, tn=2 * M_WIDTH)
        xf = xattn_sublayer(xf, xattn_norm_pre[layer], m_w_q[layer].astype(BF16), kv,
                            m_w_o[layer].astype(BF16), xattn_norm_post[layer],
                            seq=seq, n_mem=n_mem, tm=512)
    return xf.reshape(batch, seq, d)
```

```python
import functools
import math

import jax
import jax.numpy as jnp
from jax import lax
from jax.experimental import pallas as pl
from jax.experimental.pallas import tpu as pltpu

F32 = jnp.float32
BF16 = jnp.bfloat16

EPS = 1e-6
CHUNK = 64
PACK = 256
A_HEADS = 8
A_HEAD_DIM = 128
A_WIDTH = A_HEADS * A_HEAD_DIM
A_CONV = 4
B_HEADS = 8
B_QK_DIM = 64
B_V_DIM = 128
B_WIDTH = B_HEADS * B_V_DIM
ROPE_THETA = 10000.0
C_CONV = 3
M_HEADS = 4
M_HEAD_DIM = 128
M_WIDTH = M_HEADS * M_HEAD_DIM
LANES = 128
SUBLANES = 8
NEG_BIG = -1e30
DATTN_ONES_ROWS = 16
VMEM_LIMIT = 56 * 1024 * 1024


def _cparams(sem):
    return pltpu.CompilerParams(dimension_semantics=sem, vmem_limit_bytes=VMEM_LIMIT)


def _mm(a, b):
    return jnp.dot(a.astype(BF16), b.astype(BF16), preferred_element_type=F32)


def _mm_nt(a, b):
    return lax.dot_general(a.astype(BF16), b.astype(BF16), (((1,), (1,)), ((), ())),
                           preferred_element_type=F32)


def _mm_tn(a, b):
    return lax.dot_general(a.astype(BF16), b.astype(BF16), (((0,), (0,)), ((), ())),
                           preferred_element_type=F32)


def _rms(x, gain):
    return x * lax.rsqrt(jnp.mean(x * x, axis=-1, keepdims=True) + EPS) * gain


def _sigmoid(x):
    return 1.0 / (1.0 + jnp.exp(-x))


def _silu(x):
    return x * _sigmoid(x)


def _softplus(x):
    return jnp.maximum(x, 0.0) + jnp.log1p(jnp.exp(-jnp.abs(x)))


def _shifted_rows(tail, x, shift):
    ext = jnp.concatenate([tail, x], axis=0)
    return pltpu.roll(ext, shift, axis=0)[SUBLANES:]


def _norm_matmul_kernel(x_ref, g_ref, w_ref, o_ref, h_ref):
    @pl.when(pl.program_id(1) == 0)
    def _():
        h_ref[...] = _rms(x_ref[...], g_ref[...]).astype(BF16)

    o_ref[...] = jnp.dot(h_ref[...], w_ref[...], preferred_element_type=F32).astype(o_ref.dtype)


def _norm_matmul_gates_kernel(x_ref, g_ref, w_ref, wg_ref, o_ref, og_ref, h_ref):
    @pl.when(pl.program_id(1) == 0)
    def _():
        h_ref[...] = _rms(x_ref[...], g_ref[...]).astype(BF16)
        og_ref[...] = jnp.dot(h_ref[...], wg_ref[...], preferred_element_type=F32)

    o_ref[...] = jnp.dot(h_ref[...], w_ref[...], preferred_element_type=F32).astype(o_ref.dtype)


def norm_matmul(x, gain, w, *, tm, tn, w_gates=None):
    t, d = x.shape
    n = w.shape[1]
    tm = min(tm, t)
    grid = (t // tm, n // tn)
    x_spec = pl.BlockSpec((tm, d), lambda i, j: (i, 0))
    g_spec = pl.BlockSpec((1, d), lambda i, j: (0, 0))
    w_spec = pl.BlockSpec((d, tn), lambda i, j: (0, j))
    o_spec = pl.BlockSpec((tm, tn), lambda i, j: (i, j))
    scratch = [pltpu.VMEM((tm, d), BF16)]
    if w_gates is None:
        return pl.pallas_call(
            _norm_matmul_kernel, grid=grid,
            in_specs=[x_spec, g_spec, w_spec], out_specs=o_spec,
            out_shape=jax.ShapeDtypeStruct((t, n), BF16),
            scratch_shapes=scratch, compiler_params=_cparams(("parallel", "arbitrary")),
            name="norm_matmul",
        )(x, gain.reshape(1, d), w)
    return pl.pallas_call(
        _norm_matmul_gates_kernel, grid=grid,
        in_specs=[x_spec, g_spec, w_spec, pl.BlockSpec((d, LANES), lambda i, j: (0, 0))],
        out_specs=[o_spec, pl.BlockSpec((tm, LANES), lambda i, j: (i, 0))],
        out_shape=[jax.ShapeDtypeStruct((t, n), BF16), jax.ShapeDtypeStruct((t, LANES), F32)],
        scratch_shapes=scratch, compiler_params=_cparams(("parallel", "arbitrary")),
        name="norm_matmul_gates",
    )(x, gain.reshape(1, d), w, w_gates)


def _rope_kernel(q_ref, k_ref, cos_ref, sin_ref, qo_ref, ko_ref):
    cos = cos_ref[...]
    sin = sin_ref[...]
    lane = lax.broadcasted_iota(jnp.int32, cos.shape, 1)
    first_half = (lane & 32) == 0

    def rot(ref, out_ref, scale):
        for h in range(B_HEADS):
            sl = slice(h * LANES, (h + 1) * LANES)
            t = ref[:, sl].astype(F32)
            partner = jnp.where(first_half, pltpu.roll(t, LANES - 32, axis=1), pltpu.roll(t, 32, axis=1))
            out_ref[:, sl] = ((t * cos + partner * sin) * scale).astype(out_ref.dtype)

    rot(q_ref, qo_ref, B_QK_DIM ** -0.5 * math.log2(math.e))
    rot(k_ref, ko_ref, 1.0)


def rope_qk(proj, cos_t, sin_t, *, seq, tm):
    t = proj.shape[0]
    tm = min(tm, seq)
    spb = seq // tm
    qcol = 4 * A_WIDTH // B_WIDTH
    return pl.pallas_call(
        _rope_kernel, grid=(t // tm,),
        in_specs=[pl.BlockSpec((tm, B_WIDTH), lambda i: (i, qcol)),
                  pl.BlockSpec((tm, B_WIDTH), lambda i: (i, qcol + 1)),
                  pl.BlockSpec((tm, LANES), lambda i: (i % spb, 0)),
                  pl.BlockSpec((tm, LANES), lambda i: (i % spb, 0))],
        out_specs=[pl.BlockSpec((tm, B_WIDTH), lambda i: (i, 0)),
                   pl.BlockSpec((tm, B_WIDTH), lambda i: (i, 0))],
        out_shape=[jax.ShapeDtypeStruct((t, B_WIDTH), BF16)] * 2,
        compiler_params=_cparams(("parallel",)), name="rope_qk",
    )(proj, proj, cos_t, sin_t)


def _gdn_kernel(q_ref, k_ref, v_ref, z_ref, bg_ref, cwq_ref, cwk_ref, cwv_ref,
                arow_ref, dtrow_ref, anorm_ref, o_ref,
                state_ref, tq_ref, tk_ref, tv_ref, *, ts, hb):
    hg = pl.program_id(1)

    @pl.when(pl.program_id(2) == 0)
    def _():
        state_ref[...] = jnp.zeros_like(state_ref)
        tq_ref[...] = jnp.zeros_like(tq_ref)
        tk_ref[...] = jnp.zeros_like(tk_ref)
        tv_ref[...] = jnp.zeros_like(tv_ref)

    def conv_silu(x_ref, w_ref, tail_ref):
        x = x_ref[...].astype(F32)
        tail = tail_ref[...]
        w = w_ref[...]
        acc = x * w[A_CONV - 1:A_CONV]
        for j in range(A_CONV - 1):
            acc = acc + _shifted_rows(tail, x, A_CONV - 1 - j) * w[j:j + 1]
        tail_ref[...] = x[ts - SUBLANES:]
        return _silu(acc)

    q_all = conv_silu(q_ref, cwq_ref, tq_ref)
    k_all = conv_silu(k_ref, cwk_ref, tk_ref)
    v_all = conv_silu(v_ref, cwv_ref, tv_ref)

    bg = bg_ref[...]
    beta_all = _sigmoid(bg)
    dec = -jnp.exp(arow_ref[...]) * _softplus(bg + dtrow_ref[...])
    row_in_chunk = lax.broadcasted_iota(jnp.int32, (ts, 1), 0) & (CHUNK - 1)
    step = 1
    while step < CHUNK:
        dec = dec + jnp.where(row_in_chunk >= step, pltpu.roll(dec, step, axis=0), 0.0)
        step *= 2
    lane = lax.broadcasted_iota(jnp.int32, (ts, LANES), 1)

    ii = lax.broadcasted_iota(jnp.int32, (PACK, PACK), 0)
    jj = lax.broadcasted_iota(jnp.int32, (PACK, PACK), 1)
    same64 = (ii >> 6) == (jj >> 6)
    tril = same64 & (ii >= jj)
    strict = same64 & (ii > jj)
    same16 = (ii >> 4) == (jj >> 4)
    same32 = (ii >> 5) == (jj >> 5)
    eye = jnp.where(ii == jj, 1.0, 0.0)

    heads = range(hb)
    packs = range(ts // PACK)
    sls = [slice(j * A_HEAD_DIM, (j + 1) * A_HEAD_DIM) for j in heads]
    qh, kh, d, kb, rhs, qd = [], [], [], [], [], []
    for j in heads:
        head = hg * hb + j
        q_j = q_all[:, sls[j]]
        k_j = k_all[:, sls[j]]
        q_j = q_j * (lax.rsqrt(jnp.sum(q_j * q_j, axis=-1, keepdims=True) + EPS) * (A_HEAD_DIM ** -0.5))
        k_j = k_j * lax.rsqrt(jnp.sum(k_j * k_j, axis=-1, keepdims=True) + EPS)
        beta = jnp.sum(jnp.where(lane == head, beta_all, 0.0), axis=-1, keepdims=True)
        d_j = jnp.sum(jnp.where(lane == head + A_HEADS, dec, 0.0), axis=-1, keepdims=True)
        exp_d = jnp.exp(d_j)
        kb_j = k_j * beta
        qh.append(q_j)
        kh.append(k_j)
        d.append(d_j)
        kb.append(kb_j)
        rhs.append(jnp.concatenate([v_all[:, sls[j]] * beta, kb_j * exp_d], axis=-1))
        qd.append(q_j * exp_d)

    units = [(j, slice(p * PACK, (p + 1) * PACK)) for p in packs for j in heads]

    def decay_matrix(j, rows):
        dp = d[j][rows]
        drow = jnp.broadcast_to(dp, (PACK, LANES)).T[0:1, :]
        return jnp.where(tril, jnp.exp(jnp.where(tril, dp - drow, 0.0)), 0.0)

    gamma = [decay_matrix(j, rows) for j, rows in units]
    nmat = [jnp.where(strict, _mm_nt(kb[j][rows], kh[j][rows]) * g, 0.0) for (j, rows), g in zip(units, gamma)]
    qk = [_mm_nt(qh[j][rows], kh[j][rows]) * g for (j, rows), g in zip(units, gamma)]
    p1 = [jnp.where(same16, -n, 0.0) for n in nmat]
    inv = [eye + a for a in p1]
    p2 = [_mm(a, a) for a in p1]
    inv = [i + _mm(i, a) for i, a in zip(inv, p2)]
    p4 = [_mm(a, a) for a in p2]
    inv = [i + _mm(i, a) for i, a in zip(inv, p4)]
    p8 = [_mm(a, a) for a in p4]
    inv = [i + _mm(i, a) for i, a in zip(inv, p8)]
    not16 = jnp.logical_not(same16)
    t1 = [_mm(jnp.where(same32 & not16, n, 0.0), i) for n, i in zip(nmat, inv)]
    inv = [i - _mm(i, a) for i, a in zip(inv, t1)]
    t2 = [_mm(jnp.where(same32, 0.0, n), i) for n, i in zip(nmat, inv)]
    inv = [i - _mm(i, a) for i, a in zip(inv, t2)]
    sol = [_mm(i, rhs[j][rows]) for (j, rows), i in zip(units, inv)]

    state = [state_ref[j] for j in heads]
    outs = [[] for _ in heads]
    for p in packs:
        v_new = [[] for _ in heads]
        o_inter = [[] for _ in heads]
        for c in range(PACK // CHUNK):
            cr = slice(c * CHUNK, (c + 1) * CHUNK)
            gr = slice(p * PACK + c * CHUNK, p * PACK + (c + 1) * CHUNK)
            for j in heads:
                s_uw = sol[p * hb + j]
                d_c = d[j][gr]
                d_last = d_c[CHUNK - 1:CHUNK]
                vn = s_uw[cr, :A_HEAD_DIM] - _mm(s_uw[cr, A_HEAD_DIM:], state[j])
                o_inter[j].append(_mm(qd[j][gr], state[j]))
                kd = kh[j][gr] * jnp.exp(d_last - d_c)
                state[j] = state[j] * jnp.exp(d_last) + _mm_tn(kd, vn)
                v_new[j].append(vn)
        for j in heads:
            outs[j].append(jnp.concatenate(o_inter[j], axis=0)
                           + _mm(qk[p * hb + j], jnp.concatenate(v_new[j], axis=0)))
    for j in heads:
        state_ref[j] = state[j]
        o = jnp.concatenate(outs[j], axis=0) if len(outs[j]) > 1 else outs[j][0]
        y = _rms(o, anorm_ref[...]) * _silu(z_ref[:, sls[j]].astype(F32))
        o_ref[:, sls[j]] = y.astype(o_ref.dtype)


def gdn_heads(proj, gates, conv_w, a_row, dt_row, a_norm, *, batch, seq, ts, hb):
    t = proj.shape[0]
    ts = min(ts, seq)
    nst = seq // ts
    ngrp = A_HEADS // hb
    wblk = hb * A_HEAD_DIM

    def col(off):
        return pl.BlockSpec((ts, wblk), lambda b, g, s: (b * nst + s, off * ngrp + g))

    def cw(off):
        return pl.BlockSpec((A_CONV, wblk), lambda b, g, s: (0, off * ngrp + g))

    row = pl.BlockSpec((1, LANES), lambda b, g, s: (0, 0))
    return pl.pallas_call(
        functools.partial(_gdn_kernel, ts=ts, hb=hb),
        grid=(batch, ngrp, nst),
        in_specs=[col(0), col(1), col(2), col(3),
                  pl.BlockSpec((ts, LANES), lambda b, g, s: (b * nst + s, 0)),
                  cw(0), cw(1), cw(2), row, row, row],
        out_specs=pl.BlockSpec((ts, wblk), lambda b, g, s: (b * nst + s, g)),
        out_shape=jax.ShapeDtypeStruct((t, A_WIDTH), BF16),
        scratch_shapes=[pltpu.VMEM((hb, A_HEAD_DIM, A_HEAD_DIM), F32),
                        pltpu.VMEM((SUBLANES, wblk), F32),
                        pltpu.VMEM((SUBLANES, wblk), F32),
                        pltpu.VMEM((SUBLANES, wblk), F32)],
        compiler_params=_cparams(("parallel", "parallel", "arbitrary")),
        name="gdn_heads",
    )(proj, proj, proj, proj, gates, conv_w, conv_w, conv_w, a_row, dt_row, a_norm)


def _dattn_kernel(q_ref, k_ref, v_ref, z_ref, lamp_ref, bnorm_ref, o_ref,
                  qs_ref, vt_ref, m_ref, acc_ref, *, tq, tk, seq, lam_init):
    qi = pl.program_id(2)

    @pl.when(qi == 0)
    def _():
        for c in range(seq // tk):
            cs = slice(c * tk, (c + 1) * tk)
            vt_ref[0:B_V_DIM, cs] = v_ref[cs, :].astype(F32).T.astype(BF16)
        vt_ref[B_V_DIM:, :] = jnp.ones((DATTN_ONES_ROWS, seq), BF16)

    q = q_ref[...]
    lane = lax.broadcasted_iota(jnp.int32, q.shape, 1)
    zero = jnp.zeros_like(q)
    qs_ref[0:tq, :] = jnp.where(lane < B_QK_DIM, q, zero)
    qs_ref[tq:, :] = jnp.where(lane < B_QK_DIM, zero, q)
    m_ref[...] = jnp.full_like(m_ref, NEG_BIG)
    acc_ref[...] = jnp.zeros_like(acc_ref)

    def scores(start):
        return _mm_nt(k_ref[pl.ds(start, tk), :], qs_ref[...])

    def update(s_t, start):
        m_prev = m_ref[...]
        m_new = jnp.maximum(m_prev, jnp.max(s_t, axis=0, keepdims=True))
        alpha = jnp.exp2(m_prev - m_new)
        p_t = jnp.exp2(s_t - m_new).astype(BF16)
        acc_ref[...] = alpha * acc_ref[...] + jnp.dot(vt_ref[:, pl.ds(start, tk)], p_t,
                                                      preferred_element_type=F32)
        m_ref[...] = m_new

    blocks_per_q = tq // tk

    def body(i, carry):
        for u in range(blocks_per_q):
            start = pl.multiple_of((i * blocks_per_q + u) * tk, tk)
            update(scores(start), start)
        return carry

    lax.fori_loop(0, qi, body, 0)

    for u in range(blocks_per_q):
        start = pl.multiple_of(qi * tq + u * tk, tk)
        s_t = scores(start)
        key_chunk = (lax.broadcasted_iota(jnp.int32, s_t.shape, 0) + u * tk) >> 6
        q_chunk = (lax.broadcasted_iota(jnp.int32, s_t.shape, 1) & (tq - 1)) >> 6
        update(jnp.where(key_chunk <= q_chunk, s_t, NEG_BIG), start)

    lp = lamp_ref[...]
    lam = (jnp.exp(jnp.sum(lp[0:1] * lp[1:2], axis=-1, keepdims=True))
           - jnp.exp(jnp.sum(lp[2:3] * lp[3:4], axis=-1, keepdims=True)) + lam_init)
    acc = acc_ref[...]
    o_all = acc[0:B_V_DIM] / acc[B_V_DIM:B_V_DIM + 1]
    o = (o_all[:, :tq] - lam * o_all[:, tq:]).T
    y = _rms(o, bnorm_ref[...]) * (1.0 - lam_init) * _silu(z_ref[...].astype(F32))
    o_ref[...] = y.astype(o_ref.dtype)


def dattn_heads(q_rot, k_rot, proj, lam_params, b_norm, *, batch, seq, tq, tk, lam_init):
    t = q_rot.shape[0]
    tq = min(tq, seq)
    tk = min(tk, tq)
    nq = seq // tq
    vcol = (4 * A_WIDTH + 2 * B_WIDTH) // LANES
    zcol = (4 * A_WIDTH + 3 * B_WIDTH) // LANES
    return pl.pallas_call(
        functools.partial(_dattn_kernel, tq=tq, tk=tk, seq=seq, lam_init=lam_init),
        grid=(batch, B_HEADS, nq),
        in_specs=[pl.BlockSpec((tq, LANES), lambda b, h, i: (b * nq + i, h)),
                  pl.BlockSpec((seq, LANES), lambda b, h, i: (b, h)),
                  pl.BlockSpec((seq, LANES), lambda b, h, i: (b, vcol + h)),
                  pl.BlockSpec((tq, LANES), lambda b, h, i: (b * nq + i, zcol + h)),
                  pl.BlockSpec((4, B_QK_DIM), lambda b, h, i: (0, 0)),
                  pl.BlockSpec((1, LANES), lambda b, h, i: (0, 0))],
        out_specs=pl.BlockSpec((tq, LANES), lambda b, h, i: (b * nq + i, h)),
        out_shape=jax.ShapeDtypeStruct((t, B_WIDTH), BF16),
        scratch_shapes=[pltpu.VMEM((2 * tq, LANES), BF16),
                        pltpu.VMEM((B_V_DIM + DATTN_ONES_ROWS, seq), BF16),
                        pltpu.VMEM((1, 2 * tq), F32),
                        pltpu.VMEM((B_V_DIM + DATTN_ONES_ROWS, 2 * tq), F32)],
        compiler_params=_cparams(("parallel", "parallel", "arbitrary")),
        name="dattn_heads",
    )(q_rot, k_rot, proj, proj, lam_params, b_norm)


def _ab_out_kernel(ya_ref, yb_ref, wa_ref, wb_ref, g_ref, x_ref, o_ref):
    y = (jnp.dot(ya_ref[...], wa_ref[...], preferred_element_type=F32)
         + jnp.dot(yb_ref[...], wb_ref[...], preferred_element_type=F32))
    o_ref[...] = x_ref[...] + _rms(y, g_ref[...])


def ab_out_proj(y_a, y_b, w_a, w_b, gain, x, *, tm):
    t, d = x.shape
    tm = min(tm, t)
    return pl.pallas_call(
        _ab_out_kernel, grid=(t // tm,),
        in_specs=[pl.BlockSpec((tm, A_WIDTH), lambda i: (i, 0)),
                  pl.BlockSpec((tm, B_WIDTH), lambda i: (i, 0)),
                  pl.BlockSpec((A_WIDTH, d), lambda i: (0, 0)),
                  pl.BlockSpec((B_WIDTH, d), lambda i: (0, 0)),
                  pl.BlockSpec((1, d), lambda i: (0, 0)),
                  pl.BlockSpec((tm, d), lambda i: (i, 0))],
        out_specs=pl.BlockSpec((tm, d), lambda i: (i, 0)),
        out_shape=jax.ShapeDtypeStruct((t, d), F32),
        compiler_params=_cparams(("parallel",)), name="ab_out_proj",
    )(y_a, y_b, w_a, w_b, gain.reshape(1, d), x)


def _conv_out_kernel(u_ref, b_ref, c_ref, z_ref, cw_ref, w_ref, g_ref, x_ref, o_ref, tail_ref,
                     *, tm, tiles_per_seq):
    @pl.when(pl.program_id(0) % tiles_per_seq == 0)
    def _():
        tail_ref[...] = jnp.zeros_like(tail_ref)

    cu = c_ref[...].astype(F32) * u_ref[...].astype(F32)
    tail = tail_ref[...]
    cw = cw_ref[...]
    conv = cu * cw[C_CONV - 1:C_CONV]
    for j in range(C_CONV - 1):
        conv = conv + _shifted_rows(tail, cu, C_CONV - 1 - j) * cw[j:j + 1]
    tail_ref[...] = cu[tm - SUBLANES:]
    gated = b_ref[...].astype(F32) * conv * _silu(z_ref[...].astype(F32))
    y = jnp.dot(gated.astype(BF16), w_ref[...], preferred_element_type=F32)
    o_ref[...] = x_ref[...] + _rms(y, g_ref[...])


def conv_out_proj(proj, conv_w, w_out, gain, x, *, seq, tm):
    t, d = x.shape
    tm = min(tm, seq)

    def col(c):
        return pl.BlockSpec((tm, d), lambda i: (i, c))

    return pl.pallas_call(
        functools.partial(_conv_out_kernel, tm=tm, tiles_per_seq=seq // tm),
        grid=(t // tm,),
        in_specs=[col(0), col(1), col(2), col(3),
                  pl.BlockSpec((C_CONV, d), lambda i: (0, 0)),
                  pl.BlockSpec((d, d), lambda i: (0, 0)),
                  pl.BlockSpec((1, d), lambda i: (0, 0)),
                  pl.BlockSpec((tm, d), lambda i: (i, 0))],
        out_specs=pl.BlockSpec((tm, d), lambda i: (i, 0)),
        out_shape=jax.ShapeDtypeStruct((t, d), F32),
        scratch_shapes=[pltpu.VMEM((SUBLANES, d), F32)],
        compiler_params=_cparams(("arbitrary",)), name="conv_out_proj",
    )(proj, proj, proj, proj, conv_w, w_out, gain.reshape(1, d), x)


def _xattn_kernel(x_ref, gpre_ref, wq_ref, k_ref, v_ref, wo_ref, gpost_ref, o_ref):
    x = x_ref[...]
    h = _rms(x, gpre_ref[...]).astype(BF16)
    qz = jnp.dot(h, wq_ref[...], preferred_element_type=F32)
    q = qz[:, :M_WIDTH] * (M_HEAD_DIM ** -0.5)
    z = qz[:, M_WIDTH:]
    heads = []
    for hd in range(M_HEADS):
        sl = slice(hd * M_HEAD_DIM, (hd + 1) * M_HEAD_DIM)
        s = _mm_nt(q[:, sl], k_ref[:, sl])
        p = jnp.exp(s - jnp.max(s, axis=-1, keepdims=True))
        p = p / jnp.sum(p, axis=-1, keepdims=True)
        heads.append(_mm(p, v_ref[:, sl]))
    o = jnp.concatenate(heads, axis=-1) * _silu(z)
    y = jnp.dot(o.astype(BF16), wo_ref[...], preferred_element_type=F32)
    o_ref[...] = x + _rms(y, gpost_ref[...])


def xattn_sublayer(x, g_pre, w_q, kv, w_o, g_post, *, seq, n_mem, tm):
    t, d = x.shape
    tm = min(tm, seq)
    spb = seq // tm
    return pl.pallas_call(
        _xattn_kernel, grid=(t // tm,),
        in_specs=[pl.BlockSpec((tm, d), lambda i: (i, 0)),
                  pl.BlockSpec((1, d), lambda i: (0, 0)),
                  pl.BlockSpec((d, 2 * M_WIDTH), lambda i: (0, 0)),
                  pl.BlockSpec((n_mem, M_WIDTH), lambda i: (i // spb, 0)),
                  pl.BlockSpec((n_mem, M_WIDTH), lambda i: (i // spb, 1)),
                  pl.BlockSpec((M_WIDTH, d), lambda i: (0, 0)),
                  pl.BlockSpec((1, d), lambda i: (0, 0))],
        out_specs=pl.BlockSpec((tm, d), lambda i: (i, 0)),
        out_shape=jax.ShapeDtypeStruct((t, d), F32),
        compiler_params=_cparams(("parallel",)), name="xattn_sublayer",
    )(x, g_pre.reshape(1, d), w_q, kv, kv, w_o, g_post.reshape(1, d))


def _lambda_init(layer):
    return 0.8 - 0.6 * math.exp(-0.3 * layer)


def _rope_lane_tables(seq):
    half = B_QK_DIM // 2
    inv_freq = ROPE_THETA ** (-jnp.arange(0, B_QK_DIM, 2, dtype=F32) / B_QK_DIM)
    ang = jnp.arange(seq, dtype=F32)[:, None] * inv_freq[None, :]
    cos, sin = jnp.cos(ang), jnp.sin(ang)
    assert cos.shape == (seq, half)
    return jnp.tile(cos, (1, 4)), jnp.concatenate([-sin, sin, -sin, sin], axis=-1)


def _lane_row(values, offset):
    return jnp.zeros((1, LANES), F32).at[0, offset:offset + values.shape[0]].set(values.astype(F32))


def kernel(x, mem, norm_pre, norm_post, xattn_norm_pre, xattn_norm_post, mem_norm,
           ab_w_in, a_conv, a_A_log, a_dt_bias, a_out_norm, b_lambda, b_out_norm, ab_w_out,
           c_w_in, c_conv, c_w_out, m_w_q, m_w_kv, m_w_o):
    batch, seq, d = x.shape
    n_mem = mem.shape[1]
    depth = norm_pre.shape[0]
    cos_t, sin_t = _rope_lane_tables(seq)
    xf = x.reshape(batch * seq, d)
    memf = mem.reshape(batch * n_mem, d)
    gate_lo = 4 * A_WIDTH
    gate_hi = gate_lo + 2 * A_HEADS

    for layer in range(depth):
        if layer % 2 == 0:
            e = layer // 2
            w_in = ab_w_in[e]
            w_main = jnp.concatenate([w_in[:, :gate_lo], w_in[:, gate_hi:]], axis=1).astype(BF16)
            w_gates = jnp.pad(w_in[:, gate_lo:gate_hi], ((0, 0), (0, LANES - 2 * A_HEADS))).astype(BF16)
            proj, gates = norm_matmul(xf, norm_pre[layer], w_main, tm=1024, tn=1024, w_gates=w_gates)
            y_a = gdn_heads(proj, gates, a_conv[e], _lane_row(a_A_log[e], A_HEADS),
                            _lane_row(a_dt_bias[e], A_HEADS), a_out_norm[e].reshape(1, A_HEAD_DIM),
                            batch=batch, seq=seq, ts=256, hb=4)
            q_rot, k_rot = rope_qk(proj, cos_t, sin_t, seq=seq, tm=512)
            y_b = dattn_heads(q_rot, k_rot, proj, b_lambda[e], b_out_norm[e].reshape(1, B_V_DIM),
                              batch=batch, seq=seq, tq=512, tk=256, lam_init=_lambda_init(layer))
            w_out = ab_w_out[e].astype(BF16)
            xf = ab_out_proj(y_a, y_b, w_out[:A_WIDTH], w_out[A_WIDTH:], norm_post[layer], xf, tm=512)
        else:
            o = layer // 2
            proj = norm_matmul(xf, norm_pre[layer], c_w_in[o].astype(BF16), tm=1024, tn=1024)
            xf = conv_out_proj(proj, c_conv[o], c_w_out[o].astype(BF16), norm_post[layer], xf,
                               seq=seq, tm=256)
        kv = norm_matmul(memf, mem_norm[layer], m_w_kv[layer].astype(BF16), tm=512, tn=2 * M_WIDTH)
        xf = xattn_sublayer(xf, xattn_norm_pre[layer], m_w_q[layer].astype(BF16), kv,
                            m_w_o[layer].astype(BF16), xattn_norm_post[layer],
                            seq=seq, n_mem=n_mem, tm=512)
    return xf.reshape(batch, seq, d)
```

```python
import functools
import math

import jax
import jax.numpy as jnp
from jax import lax
from jax.experimental import pallas as pl
from jax.experimental.pallas import tpu as pltpu

F32 = jnp.float32
BF16 = jnp.bfloat16

EPS = 1e-6
CHUNK = 64
PACK = 256
A_HEADS = 8
A_HEAD_DIM = 128
A_WIDTH = A_HEADS * A_HEAD_DIM
A_CONV = 4
B_HEADS = 8
B_QK_DIM = 64
B_V_DIM = 128
B_WIDTH = B_HEADS * B_V_DIM
ROPE_THETA = 10000.0
C_CONV = 3
M_HEADS = 4
M_HEAD_DIM = 128
M_WIDTH = M_HEADS * M_HEAD_DIM
LANES = 128
SUBLANES = 8
NEG_BIG = -1e30
DATTN_ONES_ROWS = 16
CONV_OUT_CHUNK = 512
VMEM_LIMIT = 56 * 1024 * 1024


def _cparams(sem):
    return pltpu.CompilerParams(dimension_semantics=sem, vmem_limit_bytes=VMEM_LIMIT)


def _mm(a, b):
    return jnp.dot(a.astype(BF16), b.astype(BF16), preferred_element_type=F32)


def _mm_nt(a, b):
    return lax.dot_general(a.astype(BF16), b.astype(BF16), (((1,), (1,)), ((), ())),
                           preferred_element_type=F32)


def _mm_tn(a, b):
    return lax.dot_general(a.astype(BF16), b.astype(BF16), (((0,), (0,)), ((), ())),
                           preferred_element_type=F32)


def _rms(x, gain):
    return x * lax.rsqrt(jnp.mean(x * x, axis=-1, keepdims=True) + EPS) * gain


def _sigmoid(x):
    return 1.0 / (1.0 + jnp.exp(-x))


def _silu(x):
    return x * _sigmoid(x)


def _softplus(x):
    return jnp.maximum(x, 0.0) + jnp.log1p(jnp.exp(-jnp.abs(x)))


def _shifted_rows(tail, x, shift):
    ext = jnp.concatenate([tail, x], axis=0)
    return pltpu.roll(ext, shift, axis=0)[SUBLANES:]


def _norm_matmul_kernel(x_ref, g_ref, w_ref, o_ref, h_ref):
    @pl.when(pl.program_id(1) == 0)
    def _():
        h_ref[...] = _rms(x_ref[...], g_ref[...]).astype(BF16)

    o_ref[...] = jnp.dot(h_ref[...], w_ref[...], preferred_element_type=F32).astype(o_ref.dtype)


def _norm_matmul_gates_kernel(x_ref, g_ref, w_ref, wg_ref, o_ref, og_ref, h_ref):
    @pl.when(pl.program_id(1) == 0)
    def _():
        h_ref[...] = _rms(x_ref[...], g_ref[...]).astype(BF16)
        og_ref[...] = jnp.dot(h_ref[...], wg_ref[...], preferred_element_type=F32)

    o_ref[...] = jnp.dot(h_ref[...], w_ref[...], preferred_element_type=F32).astype(o_ref.dtype)


def norm_matmul(x, gain, w, *, tm, tn, w_gates=None):
    t, d = x.shape
    n = w.shape[1]
    tm = min(tm, t)
    grid = (t // tm, n // tn)
    x_spec = pl.BlockSpec((tm, d), lambda i, j: (i, 0))
    g_spec = pl.BlockSpec((1, d), lambda i, j: (0, 0))
    w_spec = pl.BlockSpec((d, tn), lambda i, j: (0, j))
    o_spec = pl.BlockSpec((tm, tn), lambda i, j: (i, j))
    scratch = [pltpu.VMEM((tm, d), BF16)]
    if w_gates is None:
        return pl.pallas_call(
            _norm_matmul_kernel, grid=grid,
            in_specs=[x_spec, g_spec, w_spec], out_specs=o_spec,
            out_shape=jax.ShapeDtypeStruct((t, n), BF16),
            scratch_shapes=scratch, compiler_params=_cparams(("parallel", "arbitrary")),
            name="norm_matmul",
        )(x, gain.reshape(1, d), w)
    return pl.pallas_call(
        _norm_matmul_gates_kernel, grid=grid,
        in_specs=[x_spec, g_spec, w_spec, pl.BlockSpec((d, LANES), lambda i, j: (0, 0))],
        out_specs=[o_spec, pl.BlockSpec((tm, LANES), lambda i, j: (i, 0))],
        out_shape=[jax.ShapeDtypeStruct((t, n), BF16), jax.ShapeDtypeStruct((t, LANES), F32)],
        scratch_shapes=scratch, compiler_params=_cparams(("parallel", "arbitrary")),
        name="norm_matmul_gates",
    )(x, gain.reshape(1, d), w, w_gates)


def _rope_kernel(q_ref, k_ref, cos_ref, sin_ref, qo_ref, ko_ref):
    cos = cos_ref[...]
    sin = sin_ref[...]
    lane = lax.broadcasted_iota(jnp.int32, cos.shape, 1)
    first_half = (lane & 32) == 0

    def rot(ref, out_ref, scale):
        for h in range(B_HEADS):
            sl = slice(h * LANES, (h + 1) * LANES)
            t = ref[:, sl].astype(F32)
            partner = jnp.where(first_half, pltpu.roll(t, LANES - 32, axis=1), pltpu.roll(t, 32, axis=1))
            out_ref[:, sl] = ((t * cos + partner * sin) * scale).astype(out_ref.dtype)

    rot(q_ref, qo_ref, B_QK_DIM ** -0.5 * math.log2(math.e))
    rot(k_ref, ko_ref, 1.0)


def rope_qk(proj, cos_t, sin_t, *, seq, tm):
    t = proj.shape[0]
    tm = min(tm, seq)
    spb = seq // tm
    qcol = 4 * A_WIDTH // B_WIDTH
    return pl.pallas_call(
        _rope_kernel, grid=(t // tm,),
        in_specs=[pl.BlockSpec((tm, B_WIDTH), lambda i: (i, qcol)),
                  pl.BlockSpec((tm, B_WIDTH), lambda i: (i, qcol + 1)),
                  pl.BlockSpec((tm, LANES), lambda i: (i % spb, 0)),
                  pl.BlockSpec((tm, LANES), lambda i: (i % spb, 0))],
        out_specs=[pl.BlockSpec((tm, B_WIDTH), lambda i: (i, 0)),
                   pl.BlockSpec((tm, B_WIDTH), lambda i: (i, 0))],
        out_shape=[jax.ShapeDtypeStruct((t, B_WIDTH), BF16)] * 2,
        compiler_params=_cparams(("parallel",)), name="rope_qk",
    )(proj, proj, cos_t, sin_t)


def _gdn_kernel(q_ref, k_ref, v_ref, z_ref, bg_ref, cwq_ref, cwk_ref, cwv_ref,
                arow_ref, dtrow_ref, anorm_ref, o_ref,
                state_ref, tq_ref, tk_ref, tv_ref, *, ts, hb):
    hg = pl.program_id(1)

    @pl.when(pl.program_id(2) == 0)
    def _():
        state_ref[...] = jnp.zeros_like(state_ref)
        tq_ref[...] = jnp.zeros_like(tq_ref)
        tk_ref[...] = jnp.zeros_like(tk_ref)
        tv_ref[...] = jnp.zeros_like(tv_ref)

    def conv_silu(x_ref, w_ref, tail_ref):
        x = x_ref[...].astype(F32)
        tail = tail_ref[...]
        w = w_ref[...]
        acc = x * w[A_CONV - 1:A_CONV]
        for j in range(A_CONV - 1):
            acc = acc + _shifted_rows(tail, x, A_CONV - 1 - j) * w[j:j + 1]
        tail_ref[...] = x[ts - SUBLANES:]
        return _silu(acc)

    q_all = conv_silu(q_ref, cwq_ref, tq_ref)
    k_all = conv_silu(k_ref, cwk_ref, tk_ref)
    v_all = conv_silu(v_ref, cwv_ref, tv_ref)

    bg = bg_ref[...]
    beta_all = _sigmoid(bg)
    dec = -jnp.exp(arow_ref[...]) * _softplus(bg + dtrow_ref[...])
    row_in_chunk = lax.broadcasted_iota(jnp.int32, (ts, 1), 0) & (CHUNK - 1)
    step = 1
    while step < CHUNK:
        dec = dec + jnp.where(row_in_chunk >= step, pltpu.roll(dec, step, axis=0), 0.0)
        step *= 2
    lane = lax.broadcasted_iota(jnp.int32, (ts, LANES), 1)

    ii = lax.broadcasted_iota(jnp.int32, (PACK, PACK), 0)
    jj = lax.broadcasted_iota(jnp.int32, (PACK, PACK), 1)
    same64 = (ii >> 6) == (jj >> 6)
    tril = same64 & (ii >= jj)
    strict = same64 & (ii > jj)
    same16 = (ii >> 4) == (jj >> 4)
    same32 = (ii >> 5) == (jj >> 5)
    eye = jnp.where(ii == jj, 1.0, 0.0)

    heads = range(hb)
    packs = range(ts // PACK)
    sls = [slice(j * A_HEAD_DIM, (j + 1) * A_HEAD_DIM) for j in heads]
    qh, kh, d, kb, rhs, qd = [], [], [], [], [], []
    for j in heads:
        head = hg * hb + j
        q_j = q_all[:, sls[j]]
        k_j = k_all[:, sls[j]]
        q_j = q_j * (lax.rsqrt(jnp.sum(q_j * q_j, axis=-1, keepdims=True) + EPS) * (A_HEAD_DIM ** -0.5))
        k_j = k_j * lax.rsqrt(jnp.sum(k_j * k_j, axis=-1, keepdims=True) + EPS)
        beta = jnp.sum(jnp.where(lane == head, beta_all, 0.0), axis=-1, keepdims=True)
        d_j = jnp.sum(jnp.where(lane == head + A_HEADS, dec, 0.0), axis=-1, keepdims=True)
        exp_d = jnp.exp(d_j)
        kb_j = k_j * beta
        qh.append(q_j)
        kh.append(k_j)
        d.append(d_j)
        kb.append(kb_j)
        rhs.append(jnp.concatenate([v_all[:, sls[j]] * beta, kb_j * exp_d], axis=-1))
        qd.append(q_j * exp_d)

    units = [(j, slice(p * PACK, (p + 1) * PACK)) for p in packs for j in heads]

    def decay_matrix(j, rows):
        dp = d[j][rows]
        drow = jnp.broadcast_to(dp, (PACK, LANES)).T[0:1, :]
        return jnp.where(tril, jnp.exp(jnp.where(tril, dp - drow, 0.0)), 0.0)

    gamma = [decay_matrix(j, rows) for j, rows in units]
    nmat = [jnp.where(strict, _mm_nt(kb[j][rows], kh[j][rows]) * g, 0.0) for (j, rows), g in zip(units, gamma)]
    qk = [_mm_nt(qh[j][rows], kh[j][rows]) * g for (j, rows), g in zip(units, gamma)]
    p1 = [jnp.where(same16, -n, 0.0) for n in nmat]
    inv = [eye + a for a in p1]
    p2 = [_mm(a, a) for a in p1]
    inv = [i + _mm(i, a) for i, a in zip(inv, p2)]
    p4 = [_mm(a, a) for a in p2]
    inv = [i + _mm(i, a) for i, a in zip(inv, p4)]
    p8 = [_mm(a, a) for a in p4]
    inv = [i + _mm(i, a) for i, a in zip(inv, p8)]
    not16 = jnp.logical_not(same16)
    t1 = [_mm(jnp.where(same32 & not16, n, 0.0), i) for n, i in zip(nmat, inv)]
    inv = [i - _mm(i, a) for i, a in zip(inv, t1)]
    t2 = [_mm(jnp.where(same32, 0.0, n), i) for n, i in zip(nmat, inv)]
    inv = [i - _mm(i, a) for i, a in zip(inv, t2)]
    sol = [_mm(i, rhs[j][rows]) for (j, rows), i in zip(units, inv)]

    state = [state_ref[j] for j in heads]
    outs = [[] for _ in heads]
    for p in packs:
        v_new = [[] for _ in heads]
        o_inter = [[] for _ in heads]
        for c in range(PACK // CHUNK):
            cr = slice(c * CHUNK, (c + 1) * CHUNK)
            gr = slice(p * PACK + c * CHUNK, p * PACK + (c + 1) * CHUNK)
            for j in heads:
                s_uw = sol[p * hb + j]
                d_c = d[j][gr]
                d_last = d_c[CHUNK - 1:CHUNK]
                vn = s_uw[cr, :A_HEAD_DIM] - _mm(s_uw[cr, A_HEAD_DIM:], state[j])
                o_inter[j].append(_mm(qd[j][gr], state[j]))
                kd = kh[j][gr] * jnp.exp(d_last - d_c)
                state[j] = state[j] * jnp.exp(d_last) + _mm_tn(kd, vn)
                v_new[j].append(vn)
        for j in heads:
            outs[j].append(jnp.concatenate(o_inter[j], axis=0)
                           + _mm(qk[p * hb + j], jnp.concatenate(v_new[j], axis=0)))
    for j in heads:
        state_ref[j] = state[j]
        o = jnp.concatenate(outs[j], axis=0) if len(outs[j]) > 1 else outs[j][0]
        y = _rms(o, anorm_ref[...]) * _silu(z_ref[:, sls[j]].astype(F32))
        o_ref[:, sls[j]] = y.astype(o_ref.dtype)


def gdn_heads(proj, gates, conv_w, a_row, dt_row, a_norm, *, batch, seq, ts, hb):
    t = proj.shape[0]
    ts = min(ts, seq)
    nst = seq // ts
    ngrp = A_HEADS // hb
    wblk = hb * A_HEAD_DIM

    def col(off):
        return pl.BlockSpec((ts, wblk), lambda b, g, s: (b * nst + s, off * ngrp + g))

    def cw(off):
        return pl.BlockSpec((A_CONV, wblk), lambda b, g, s: (0, off * ngrp + g))

    row = pl.BlockSpec((1, LANES), lambda b, g, s: (0, 0))
    return pl.pallas_call(
        functools.partial(_gdn_kernel, ts=ts, hb=hb),
        grid=(batch, ngrp, nst),
        in_specs=[col(0), col(1), col(2), col(3),
                  pl.BlockSpec((ts, LANES), lambda b, g, s: (b * nst + s, 0)),
                  cw(0), cw(1), cw(2), row, row, row],
        out_specs=pl.BlockSpec((ts, wblk), lambda b, g, s: (b * nst + s, g)),
        out_shape=jax.ShapeDtypeStruct((t, A_WIDTH), BF16),
        scratch_shapes=[pltpu.VMEM((hb, A_HEAD_DIM, A_HEAD_DIM), F32),
                        pltpu.VMEM((SUBLANES, wblk), F32),
                        pltpu.VMEM((SUBLANES, wblk), F32),
                        pltpu.VMEM((SUBLANES, wblk), F32)],
        compiler_params=_cparams(("parallel", "parallel", "arbitrary")),
        name="gdn_heads",
    )(proj, proj, proj, proj, gates, conv_w, conv_w, conv_w, a_row, dt_row, a_norm)


def _dattn_kernel(q_ref, k_ref, v_ref, z_ref, lamp_ref, bnorm_ref, o_ref,
                  qs_ref, vt_ref, m_ref, acc_ref, s_ref, *, tq, tk, seq, lam_init):
    qi = pl.program_id(2)

    @pl.when(qi == 0)
    def _():
        for c in range(seq // tk):
            cs = slice(c * tk, (c + 1) * tk)
            vt_ref[0:B_V_DIM, cs] = v_ref[cs, :].astype(F32).T.astype(BF16)
        vt_ref[B_V_DIM:, :] = jnp.ones((DATTN_ONES_ROWS, seq), BF16)

    q = q_ref[...]
    lane = lax.broadcasted_iota(jnp.int32, q.shape, 1)
    zero = jnp.zeros_like(q)
    qs_ref[0:tq, :] = jnp.where(lane < B_QK_DIM, q, zero)
    qs_ref[tq:, :] = jnp.where(lane < B_QK_DIM, zero, q)
    m_ref[...] = jnp.full_like(m_ref, NEG_BIG)
    acc_ref[...] = jnp.zeros_like(acc_ref)

    def scores(start):
        return _mm_nt(k_ref[pl.ds(start, tk), :], qs_ref[...])

    def update(s_t, start):
        m_prev = m_ref[...]
        m_new = jnp.maximum(m_prev, jnp.max(s_t, axis=0, keepdims=True))
        alpha = jnp.exp2(m_prev - m_new)
        p_t = jnp.exp2(s_t - m_new).astype(BF16)
        acc_ref[...] = alpha * acc_ref[...] + jnp.dot(vt_ref[:, pl.ds(start, tk)], p_t,
                                                      preferred_element_type=F32)
        m_ref[...] = m_new

    assert tq == 2 * tk

    def block_start(n):
        return pl.multiple_of(n * tk, tk)

    s_ref[...] = scores(block_start(0))

    def body(i, carry):
        s_even = s_ref[...]
        s_odd = scores(block_start(2 * i + 1))
        update(s_even, block_start(2 * i))
        s_ref[...] = scores(block_start(2 * i + 2))
        update(s_odd, block_start(2 * i + 1))
        return carry

    lax.fori_loop(0, qi, body, 0)

    def causal(s_t, u):
        key_chunk = (lax.broadcasted_iota(jnp.int32, s_t.shape, 0) + u * tk) >> 6
        q_chunk = (lax.broadcasted_iota(jnp.int32, s_t.shape, 1) & (tq - 1)) >> 6
        return jnp.where(key_chunk <= q_chunk, s_t, NEG_BIG)

    s_odd = scores(block_start(2 * qi + 1))
    update(causal(s_ref[...], 0), block_start(2 * qi))
    update(causal(s_odd, 1), block_start(2 * qi + 1))

    lp = lamp_ref[...]
    lam = (jnp.exp(jnp.sum(lp[0:1] * lp[1:2], axis=-1, keepdims=True))
           - jnp.exp(jnp.sum(lp[2:3] * lp[3:4], axis=-1, keepdims=True)) + lam_init)
    acc = acc_ref[...]
    o_all = acc[0:B_V_DIM] / acc[B_V_DIM:B_V_DIM + 1]
    o = (o_all[:, :tq] - lam * o_all[:, tq:]).T
    y = _rms(o, bnorm_ref[...]) * (1.0 - lam_init) * _silu(z_ref[...].astype(F32))
    o_ref[...] = y.astype(o_ref.dtype)


def dattn_heads(q_rot, k_rot, proj, lam_params, b_norm, *, batch, seq, tq, tk, lam_init):
    t = q_rot.shape[0]
    tq = min(tq, seq)
    tk = min(tk, tq)
    nq = seq // tq
    vcol = (4 * A_WIDTH + 2 * B_WIDTH) // LANES
    zcol = (4 * A_WIDTH + 3 * B_WIDTH) // LANES
    return pl.pallas_call(
        functools.partial(_dattn_kernel, tq=tq, tk=tk, seq=seq, lam_init=lam_init),
        grid=(batch, B_HEADS, nq),
        in_specs=[pl.BlockSpec((tq, LANES), lambda b, h, i: (b * nq + i, h)),
                  pl.BlockSpec((seq, LANES), lambda b, h, i: (b, h)),
                  pl.BlockSpec((seq, LANES), lambda b, h, i: (b, vcol + h)),
                  pl.BlockSpec((tq, LANES), lambda b, h, i: (b * nq + i, zcol + h)),
                  pl.BlockSpec((4, B_QK_DIM), lambda b, h, i: (0, 0)),
                  pl.BlockSpec((1, LANES), lambda b, h, i: (0, 0))],
        out_specs=pl.BlockSpec((tq, LANES), lambda b, h, i: (b * nq + i, h)),
        out_shape=jax.ShapeDtypeStruct((t, B_WIDTH), BF16),
        scratch_shapes=[pltpu.VMEM((2 * tq, LANES), BF16),
                        pltpu.VMEM((B_V_DIM + DATTN_ONES_ROWS, seq), BF16),
                        pltpu.VMEM((1, 2 * tq), F32),
                        pltpu.VMEM((B_V_DIM + DATTN_ONES_ROWS, 2 * tq), F32),
                        pltpu.VMEM((tk, 2 * tq), F32)],
        compiler_params=_cparams(("parallel", "parallel", "arbitrary")),
        name="dattn_heads",
    )(q_rot, k_rot, proj, proj, lam_params, b_norm)


def _ab_out_kernel(ya_ref, yb_ref, wa_ref, wb_ref, g_ref, x_ref, o_ref):
    y = (jnp.dot(ya_ref[...], wa_ref[...], preferred_element_type=F32)
         + jnp.dot(yb_ref[...], wb_ref[...], preferred_element_type=F32))
    o_ref[...] = x_ref[...] + _rms(y, g_ref[...])


def ab_out_proj(y_a, y_b, w_a, w_b, gain, x, *, tm):
    t, d = x.shape
    tm = min(tm, t)
    return pl.pallas_call(
        _ab_out_kernel, grid=(t // tm,),
        in_specs=[pl.BlockSpec((tm, A_WIDTH), lambda i: (i, 0)),
                  pl.BlockSpec((tm, B_WIDTH), lambda i: (i, 0)),
                  pl.BlockSpec((A_WIDTH, d), lambda i: (0, 0)),
                  pl.BlockSpec((B_WIDTH, d), lambda i: (0, 0)),
                  pl.BlockSpec((1, d), lambda i: (0, 0)),
                  pl.BlockSpec((tm, d), lambda i: (i, 0))],
        out_specs=pl.BlockSpec((tm, d), lambda i: (i, 0)),
        out_shape=jax.ShapeDtypeStruct((t, d), F32),
        compiler_params=_cparams(("parallel",)), name="ab_out_proj",
    )(y_a, y_b, w_a, w_b, gain.reshape(1, d), x)


def _conv_out_kernel(u_ref, b_ref, c_ref, z_ref, cw_ref, w_ref, g_ref, x_ref, o_ref, tail_ref,
                     *, tm, tiles_per_seq):
    @pl.when(pl.program_id(0) % tiles_per_seq == 0)
    def _():
        tail_ref[...] = jnp.zeros_like(tail_ref)

    width = u_ref.shape[1]
    y = None
    for c0 in range(0, width, CONV_OUT_CHUNK):
        cs = slice(c0, c0 + CONV_OUT_CHUNK)
        cu = c_ref[:, cs].astype(F32) * u_ref[:, cs].astype(F32)
        tail = tail_ref[:, cs]
        cw = cw_ref[:, cs]
        conv = cu * cw[C_CONV - 1:C_CONV]
        for j in range(C_CONV - 1):
            conv = conv + _shifted_rows(tail, cu, C_CONV - 1 - j) * cw[j:j + 1]
        tail_ref[:, cs] = cu[tm - SUBLANES:]
        gated = b_ref[:, cs].astype(F32) * conv * _silu(z_ref[:, cs].astype(F32))
        part = jnp.dot(gated.astype(BF16), w_ref[cs, :], preferred_element_type=F32)
        y = part if y is None else y + part
    o_ref[...] = x_ref[...] + _rms(y, g_ref[...])


def conv_out_proj(proj, conv_w, w_out, gain, x, *, seq, tm):
    t, d = x.shape
    tm = min(tm, seq)

    def col(c):
        return pl.BlockSpec((tm, d), lambda i: (i, c))

    return pl.pallas_call(
        functools.partial(_conv_out_kernel, tm=tm, tiles_per_seq=seq // tm),
        grid=(t // tm,),
        in_specs=[col(0), col(1), col(2), col(3),
                  pl.BlockSpec((C_CONV, d), lambda i: (0, 0)),
                  pl.BlockSpec((d, d), lambda i: (0, 0), pipeline_mode=pl.Buffered(1)),
                  pl.BlockSpec((1, d), lambda i: (0, 0)),
                  pl.BlockSpec((tm, d), lambda i: (i, 0))],
        out_specs=pl.BlockSpec((tm, d), lambda i: (i, 0)),
        out_shape=jax.ShapeDtypeStruct((t, d), F32),
        scratch_shapes=[pltpu.VMEM((SUBLANES, d), F32)],
        compiler_params=_cparams(("arbitrary",)), name="conv_out_proj",
    )(proj, proj, proj, proj, conv_w, w_out, gain.reshape(1, d), x)


def _xattn_kernel(x_ref, gpre_ref, wq_ref, k_ref, v_ref, wo_ref, gpost_ref, o_ref):
    x = x_ref[...]
    h = _rms(x, gpre_ref[...]).astype(BF16)
    qz = jnp.dot(h, wq_ref[...], preferred_element_type=F32)
    q = qz[:, :M_WIDTH] * (M_HEAD_DIM ** -0.5)
    z = qz[:, M_WIDTH:]
    heads = []
    for hd in range(M_HEADS):
        sl = slice(hd * M_HEAD_DIM, (hd + 1) * M_HEAD_DIM)
        s = _mm_nt(q[:, sl], k_ref[:, sl])
        p = jnp.exp(s - jnp.max(s, axis=-1, keepdims=True))
        p = p / jnp.sum(p, axis=-1, keepdims=True)
        heads.append(_mm(p, v_ref[:, sl]))
    o = jnp.concatenate(heads, axis=-1) * _silu(z)
    y = jnp.dot(o.astype(BF16), wo_ref[...], preferred_element_type=F32)
    o_ref[...] = x + _rms(y, gpost_ref[...])


def xattn_sublayer(x, g_pre, w_q, kv, w_o, g_post, *, seq, n_mem, tm):
    t, d = x.shape
    tm = min(tm, seq)
    spb = seq // tm
    return pl.pallas_call(
        _xattn_kernel, grid=(t // tm,),
        in_specs=[pl.BlockSpec((tm, d), lambda i: (i, 0)),
                  pl.BlockSpec((1, d), lambda i: (0, 0)),
                  pl.BlockSpec((d, 2 * M_WIDTH), lambda i: (0, 0)),
                  pl.BlockSpec((n_mem, M_WIDTH), lambda i: (i // spb, 0)),
                  pl.BlockSpec((n_mem, M_WIDTH), lambda i: (i // spb, 1)),
                  pl.BlockSpec((M_WIDTH, d), lambda i: (0, 0)),
                  pl.BlockSpec((1, d), lambda i: (0, 0))],
        out_specs=pl.BlockSpec((tm, d), lambda i: (i, 0)),
        out_shape=jax.ShapeDtypeStruct((t, d), F32),
        compiler_params=_cparams(("parallel",)), name="xattn_sublayer",
    )(x, g_pre.reshape(1, d), w_q, kv, kv, w_o, g_post.reshape(1, d))


def _lambda_init(layer):
    return 0.8 - 0.6 * math.exp(-0.3 * layer)


def _rope_lane_tables(seq):
    half = B_QK_DIM // 2
    inv_freq = ROPE_THETA ** (-jnp.arange(0, B_QK_DIM, 2, dtype=F32) / B_QK_DIM)
    ang = jnp.arange(seq, dtype=F32)[:, None] * inv_freq[None, :]
    cos, sin = jnp.cos(ang), jnp.sin(ang)
    assert cos.shape == (seq, half)
    return jnp.tile(cos, (1, 4)), jnp.concatenate([-sin, sin, -sin, sin], axis=-1)


def _lane_row(values, offset):
    return jnp.zeros((1, LANES), F32).at[0, offset:offset + values.shape[0]].set(values.astype(F32))


def kernel(x, mem, norm_pre, norm_post, xattn_norm_pre, xattn_norm_post, mem_norm,
           ab_w_in, a_conv, a_A_log, a_dt_bias, a_out_norm, b_lambda, b_out_norm, ab_w_out,
           c_w_in, c_conv, c_w_out, m_w_q, m_w_kv, m_w_o):
    batch, seq, d = x.shape
    n_mem = mem.shape[1]
    depth = norm_pre.shape[0]
    cos_t, sin_t = _rope_lane_tables(seq)
    xf = x.reshape(batch * seq, d)
    memf = mem.reshape(batch * n_mem, d)
    gate_lo = 4 * A_WIDTH
    gate_hi = gate_lo + 2 * A_HEADS

    for layer in range(depth):
        if layer % 2 == 0:
            e = layer // 2
            w_in = ab_w_in[e]
            w_main = jnp.concatenate([w_in[:, :gate_lo], w_in[:, gate_hi:]], axis=1).astype(BF16)
            w_gates = jnp.pad(w_in[:, gate_lo:gate_hi], ((0, 0), (0, LANES - 2 * A_HEADS))).astype(BF16)
            proj, gates = norm_matmul(xf, norm_pre[layer], w_main, tm=1024, tn=1024, w_gates=w_gates)
            y_a = gdn_heads(proj, gates, a_conv[e], _lane_row(a_A_log[e], A_HEADS),
                            _lane_row(a_dt_bias[e], A_HEADS), a_out_norm[e].reshape(1, A_HEAD_DIM),
                            batch=batch, seq=seq, ts=256, hb=8)
            q_rot, k_rot = rope_qk(proj, cos_t, sin_t, seq=seq, tm=512)
            y_b = dattn_heads(q_rot, k_rot, proj, b_lambda[e], b_out_norm[e].reshape(1, B_V_DIM),
                              batch=batch, seq=seq, tq=512, tk=256, lam_init=_lambda_init(layer))
            w_out = ab_w_out[e].astype(BF16)
            xf = ab_out_proj(y_a, y_b, w_out[:A_WIDTH], w_out[A_WIDTH:], norm_post[layer], xf, tm=512)
        else:
            o = layer // 2
            proj = norm_matmul(xf, norm_pre[layer], c_w_in[o].astype(BF16), tm=1024, tn=1024)
            xf = conv_out_proj(proj, c_conv[o], c_w_out[o].astype(BF16), norm_post[layer], xf,
                               seq=seq, tm=512)
        kv = norm_matmul(memf, mem_norm[layer], m_w_kv[layer].astype(BF16), tm=512, tn=2 * M_WIDTH)
        xf = xattn_sublayer(xf, xattn_norm_pre[layer], m_w_q[layer].astype(BF16), kv,
                            m_w_o[layer].astype(BF16), xattn_norm_post[layer],
                            seq=seq, n_mem=n_mem, tm=512)
    return xf.reshape(batch, seq, d)
```

```python
import functools
import math

import jax
import jax.numpy as jnp
from jax import lax
from jax.experimental import pallas as pl
from jax.experimental.pallas import tpu as pltpu

F32 = jnp.float32
BF16 = jnp.bfloat16

EPS = 1e-6
CHUNK = 64
PACK = 256
A_HEADS = 8
A_HEAD_DIM = 128
A_WIDTH = A_HEADS * A_HEAD_DIM
A_CONV = 4
B_HEADS = 8
B_QK_DIM = 64
B_V_DIM = 128
B_WIDTH = B_HEADS * B_V_DIM
ROPE_THETA = 10000.0
C_CONV = 3
M_HEADS = 4
M_HEAD_DIM = 128
M_WIDTH = M_HEADS * M_HEAD_DIM
LANES = 128
SUBLANES = 8
NEG_BIG = -1e30
DATTN_ONES_ROWS = 16
CAST_ROWS = 256
CONV_OUT_CHUNK = 512
VMEM_LIMIT = 56 * 1024 * 1024


def _cparams(sem):
    return pltpu.CompilerParams(dimension_semantics=sem, vmem_limit_bytes=VMEM_LIMIT)


def _mm(a, b):
    return jnp.dot(a.astype(BF16), b.astype(BF16), preferred_element_type=F32)


def _mm_nt(a, b):
    return lax.dot_general(a.astype(BF16), b.astype(BF16), (((1,), (1,)), ((), ())),
                           preferred_element_type=F32)


def _mm_tn(a, b):
    return lax.dot_general(a.astype(BF16), b.astype(BF16), (((0,), (0,)), ((), ())),
                           preferred_element_type=F32)


def _rms(x, gain):
    return x * lax.rsqrt(jnp.mean(x * x, axis=-1, keepdims=True) + EPS) * gain


def _sigmoid(x):
    return 0.5 * jnp.tanh(0.5 * x) + 0.5


def _silu(x):
    h = 0.5 * x
    return h + h * jnp.tanh(h)


def _softplus(x):
    return jnp.maximum(x, 0.0) + jnp.log1p(jnp.exp(-jnp.abs(x)))


def _cast_rows(src_ref, dst_ref):
    def body(i, carry):
        r = pl.multiple_of(i * CAST_ROWS, CAST_ROWS)
        dst_ref[pl.ds(r, CAST_ROWS), :] = src_ref[pl.ds(r, CAST_ROWS), :].astype(dst_ref.dtype)
        return carry

    lax.fori_loop(0, src_ref.shape[0] // CAST_ROWS, body, 0)


def _shifted_rows(tail, x, shift):
    ext = jnp.concatenate([tail, x], axis=0)
    return pltpu.roll(ext, shift, axis=0)[SUBLANES:]


def _norm_matmul_kernel(x_ref, g_ref, w_ref, o_ref, h_ref):
    @pl.when(pl.program_id(1) == 0)
    def _():
        h_ref[...] = _rms(x_ref[...], g_ref[...]).astype(BF16)

    o_ref[...] = jnp.dot(h_ref[...], w_ref[...].astype(BF16), preferred_element_type=F32).astype(o_ref.dtype)


def _norm_matmul_gates_kernel(x_ref, g_ref, w_ref, wg_ref, o_ref, og_ref, h_ref):
    @pl.when(pl.program_id(1) == 0)
    def _():
        h_ref[...] = _rms(x_ref[...], g_ref[...]).astype(BF16)
        og_ref[...] = jnp.dot(h_ref[...], wg_ref[...], preferred_element_type=F32)

    o_ref[...] = jnp.dot(h_ref[...], w_ref[...], preferred_element_type=F32).astype(o_ref.dtype)


def norm_matmul(x, gain, w, *, tm, tn, w_gates=None):
    t, d = x.shape
    n = w.shape[1]
    tm = min(tm, t)
    grid = (t // tm, n // tn)
    x_spec = pl.BlockSpec((tm, d), lambda i, j: (i, 0))
    g_spec = pl.BlockSpec((1, d), lambda i, j: (0, 0))
    w_spec = pl.BlockSpec((d, tn), lambda i, j: (0, j))
    o_spec = pl.BlockSpec((tm, tn), lambda i, j: (i, j))
    scratch = [pltpu.VMEM((tm, d), BF16)]
    if w_gates is None:
        return pl.pallas_call(
            _norm_matmul_kernel, grid=grid,
            in_specs=[x_spec, g_spec, w_spec], out_specs=o_spec,
            out_shape=jax.ShapeDtypeStruct((t, n), BF16),
            scratch_shapes=scratch, compiler_params=_cparams(("parallel", "arbitrary")),
            name="norm_matmul",
        )(x, gain.reshape(1, d), w)
    return pl.pallas_call(
        _norm_matmul_gates_kernel, grid=grid,
        in_specs=[x_spec, g_spec, w_spec, pl.BlockSpec((d, LANES), lambda i, j: (0, 0))],
        out_specs=[o_spec, pl.BlockSpec((tm, LANES), lambda i, j: (i, 0))],
        out_shape=[jax.ShapeDtypeStruct((t, n), BF16), jax.ShapeDtypeStruct((t, LANES), F32)],
        scratch_shapes=scratch, compiler_params=_cparams(("parallel", "arbitrary")),
        name="norm_matmul_gates",
    )(x, gain.reshape(1, d), w, w_gates)


def _rope_kernel(q_ref, k_ref, cos_ref, sin_ref, qo_ref, ko_ref):
    cos = cos_ref[...]
    sin = sin_ref[...]
    lane = lax.broadcasted_iota(jnp.int32, cos.shape, 1)
    first_half = (lane & 32) == 0

    def rot(ref, out_ref, scale):
        for h in range(B_HEADS):
            sl = slice(h * LANES, (h + 1) * LANES)
            t = ref[:, sl].astype(F32)
            partner = jnp.where(first_half, pltpu.roll(t, LANES - 32, axis=1), pltpu.roll(t, 32, axis=1))
            out_ref[:, sl] = ((t * cos + partner * sin) * scale).astype(out_ref.dtype)

    rot(q_ref, qo_ref, B_QK_DIM ** -0.5 * math.log2(math.e))
    rot(k_ref, ko_ref, 1.0)


def rope_qk(proj, cos_t, sin_t, *, seq, tm):
    t = proj.shape[0]
    tm = min(tm, seq)
    spb = seq // tm
    qcol = 4 * A_WIDTH // B_WIDTH
    return pl.pallas_call(
        _rope_kernel, grid=(t // tm,),
        in_specs=[pl.BlockSpec((tm, B_WIDTH), lambda i: (i, qcol)),
                  pl.BlockSpec((tm, B_WIDTH), lambda i: (i, qcol + 1)),
                  pl.BlockSpec((tm, LANES), lambda i: (i % spb, 0)),
                  pl.BlockSpec((tm, LANES), lambda i: (i % spb, 0))],
        out_specs=[pl.BlockSpec((tm, B_WIDTH), lambda i: (i, 0)),
                   pl.BlockSpec((tm, B_WIDTH), lambda i: (i, 0))],
        out_shape=[jax.ShapeDtypeStruct((t, B_WIDTH), BF16)] * 2,
        compiler_params=_cparams(("parallel",)), name="rope_qk",
    )(proj, proj, cos_t, sin_t)


def _gdn_kernel(q_ref, k_ref, v_ref, z_ref, bg_ref, cwq_ref, cwk_ref, cwv_ref,
                arow_ref, dtrow_ref, anorm_ref, o_ref,
                state_ref, tq_ref, tk_ref, tv_ref, *, ts, hb):
    hg = pl.program_id(1)

    @pl.when(pl.program_id(2) == 0)
    def _():
        state_ref[...] = jnp.zeros_like(state_ref)
        tq_ref[...] = jnp.zeros_like(tq_ref)
        tk_ref[...] = jnp.zeros_like(tk_ref)
        tv_ref[...] = jnp.zeros_like(tv_ref)

    def conv_silu(x_ref, w_ref, tail_ref):
        x = x_ref[...].astype(F32)
        tail = tail_ref[...]
        w = w_ref[...]
        acc = x * w[A_CONV - 1:A_CONV]
        for j in range(A_CONV - 1):
            acc = acc + _shifted_rows(tail, x, A_CONV - 1 - j) * w[j:j + 1]
        tail_ref[...] = x[ts - SUBLANES:]
        return _silu(acc)

    q_all = conv_silu(q_ref, cwq_ref, tq_ref)
    k_all = conv_silu(k_ref, cwk_ref, tk_ref)
    v_all = conv_silu(v_ref, cwv_ref, tv_ref)

    bg = bg_ref[...]
    beta_all = _sigmoid(bg)
    dec = -jnp.exp(arow_ref[...]) * _softplus(bg + dtrow_ref[...])
    row_in_chunk = lax.broadcasted_iota(jnp.int32, (ts, 1), 0) & (CHUNK - 1)
    step = 1
    while step < CHUNK:
        dec = dec + jnp.where(row_in_chunk >= step, pltpu.roll(dec, step, axis=0), 0.0)
        step *= 2
    lane = lax.broadcasted_iota(jnp.int32, (ts, LANES), 1)

    ii = lax.broadcasted_iota(jnp.int32, (PACK, PACK), 0)
    jj = lax.broadcasted_iota(jnp.int32, (PACK, PACK), 1)
    same64 = (ii >> 6) == (jj >> 6)
    tril = same64 & (ii >= jj)
    strict = same64 & (ii > jj)
    same16 = (ii >> 4) == (jj >> 4)
    same32 = (ii >> 5) == (jj >> 5)
    eye = jnp.where(ii == jj, 1.0, 0.0)

    heads = range(hb)
    packs = range(ts // PACK)
    sls = [slice(j * A_HEAD_DIM, (j + 1) * A_HEAD_DIM) for j in heads]
    qh, kh, d, kb, rhs, qd = [], [], [], [], [], []
    for j in heads:
        head = hg * hb + j
        q_j = q_all[:, sls[j]]
        k_j = k_all[:, sls[j]]
        q_j = q_j * (lax.rsqrt(jnp.sum(q_j * q_j, axis=-1, keepdims=True) + EPS) * (A_HEAD_DIM ** -0.5))
        k_j = k_j * lax.rsqrt(jnp.sum(k_j * k_j, axis=-1, keepdims=True) + EPS)
        beta = jnp.sum(jnp.where(lane == head, beta_all, 0.0), axis=-1, keepdims=True)
        d_j = jnp.sum(jnp.where(lane == head + A_HEADS, dec, 0.0), axis=-1, keepdims=True)
        exp_d = jnp.exp(d_j)
        kb_j = k_j * beta
        qh.append(q_j)
        kh.append(k_j)
        d.append(d_j)
        kb.append(kb_j)
        rhs.append(jnp.concatenate([v_all[:, sls[j]] * beta, kb_j * exp_d], axis=-1))
        qd.append(q_j * exp_d)

    units = [(j, slice(p * PACK, (p + 1) * PACK)) for p in packs for j in heads]

    def decay_matrix(j, rows):
        dp = d[j][rows]
        drow = jnp.broadcast_to(dp, (PACK, LANES)).T[0:1, :]
        return jnp.where(tril, jnp.exp(jnp.where(tril, dp - drow, 0.0)), 0.0)

    gamma = [decay_matrix(j, rows) for j, rows in units]
    nmat = [jnp.where(strict, _mm_nt(kb[j][rows], kh[j][rows]) * g, 0.0) for (j, rows), g in zip(units, gamma)]
    qk = [_mm_nt(qh[j][rows], kh[j][rows]) * g for (j, rows), g in zip(units, gamma)]
    p1 = [jnp.where(same16, -n, 0.0) for n in nmat]
    inv = [eye + a for a in p1]
    p2 = [_mm(a, a) for a in p1]
    inv = [i + _mm(i, a) for i, a in zip(inv, p2)]
    p4 = [_mm(a, a) for a in p2]
    inv = [i + _mm(i, a) for i, a in zip(inv, p4)]
    p8 = [_mm(a, a) for a in p4]
    inv = [i + _mm(i, a) for i, a in zip(inv, p8)]
    not16 = jnp.logical_not(same16)
    t1 = [_mm(jnp.where(same32 & not16, n, 0.0), i) for n, i in zip(nmat, inv)]
    inv = [i - _mm(i, a) for i, a in zip(inv, t1)]
    t2 = [_mm(jnp.where(same32, 0.0, n), i) for n, i in zip(nmat, inv)]
    inv = [i - _mm(i, a) for i, a in zip(inv, t2)]
    sol = [_mm(i, rhs[j][rows]) for (j, rows), i in zip(units, inv)]

    state = [state_ref[j] for j in heads]
    outs = [[] for _ in heads]
    for p in packs:
        v_new = [[] for _ in heads]
        o_inter = [[] for _ in heads]
        for c in range(PACK // CHUNK):
            cr = slice(c * CHUNK, (c + 1) * CHUNK)
            gr = slice(p * PACK + c * CHUNK, p * PACK + (c + 1) * CHUNK)
            for j in heads:
                s_uw = sol[p * hb + j]
                d_c = d[j][gr]
                d_last = d_c[CHUNK - 1:CHUNK]
                vn = s_uw[cr, :A_HEAD_DIM] - _mm(s_uw[cr, A_HEAD_DIM:], state[j])
                o_inter[j].append(_mm(qd[j][gr], state[j]))
                kd = kh[j][gr] * jnp.exp(d_last - d_c)
                state[j] = state[j] * jnp.exp(d_last) + _mm_tn(kd, vn)
                v_new[j].append(vn)
        for j in heads:
            outs[j].append(jnp.concatenate(o_inter[j], axis=0)
                           + _mm(qk[p * hb + j], jnp.concatenate(v_new[j], axis=0)))
    for j in heads:
        state_ref[j] = state[j]
        o = jnp.concatenate(outs[j], axis=0) if len(outs[j]) > 1 else outs[j][0]
        y = _rms(o, anorm_ref[...]) * _silu(z_ref[:, sls[j]].astype(F32))
        o_ref[:, sls[j]] = y.astype(o_ref.dtype)


def gdn_heads(proj, gates, conv_w, a_row, dt_row, a_norm, *, batch, seq, ts, hb):
    t = proj.shape[0]
    ts = min(ts, seq)
    nst = seq // ts
    ngrp = A_HEADS // hb
    wblk = hb * A_HEAD_DIM

    def col(off):
        return pl.BlockSpec((ts, wblk), lambda b, g, s: (b * nst + s, off * ngrp + g))

    def cw(off):
        return pl.BlockSpec((A_CONV, wblk), lambda b, g, s: (0, off * ngrp + g))

    row = pl.BlockSpec((1, LANES), lambda b, g, s: (0, 0))
    return pl.pallas_call(
        functools.partial(_gdn_kernel, ts=ts, hb=hb),
        grid=(batch, ngrp, nst),
        in_specs=[col(0), col(1), col(2), col(3),
                  pl.BlockSpec((ts, LANES), lambda b, g, s: (b * nst + s, 0)),
                  cw(0), cw(1), cw(2), row, row, row],
        out_specs=pl.BlockSpec((ts, wblk), lambda b, g, s: (b * nst + s, g)),
        out_shape=jax.ShapeDtypeStruct((t, A_WIDTH), BF16),
        scratch_shapes=[pltpu.VMEM((hb, A_HEAD_DIM, A_HEAD_DIM), F32),
                        pltpu.VMEM((SUBLANES, wblk), F32),
                        pltpu.VMEM((SUBLANES, wblk), F32),
                        pltpu.VMEM((SUBLANES, wblk), F32)],
        compiler_params=_cparams(("parallel", "parallel", "arbitrary")),
        name="gdn_heads",
    )(proj, proj, proj, proj, gates, conv_w, conv_w, conv_w, a_row, dt_row, a_norm)


def _dattn_kernel(q_ref, k_ref, v_ref, z_ref, lamp_ref, bnorm_ref, o_ref,
                  qs_ref, vt_ref, m_ref, acc_ref, s_ref, *, tq, tk, seq, lam_init):
    qi = pl.program_id(2)

    @pl.when(qi == 0)
    def _():
        for c in range(seq // tk):
            cs = slice(c * tk, (c + 1) * tk)
            vt_ref[0:B_V_DIM, cs] = v_ref[cs, :].astype(F32).T.astype(BF16)
        vt_ref[B_V_DIM:, :] = jnp.ones((DATTN_ONES_ROWS, seq), BF16)

    q = q_ref[...]
    lane = lax.broadcasted_iota(jnp.int32, q.shape, 1)
    zero = jnp.zeros_like(q)
    qs_ref[0:tq, :] = jnp.where(lane < B_QK_DIM, q, zero)
    qs_ref[tq:, :] = jnp.where(lane < B_QK_DIM, zero, q)
    m_ref[...] = jnp.full_like(m_ref, NEG_BIG)
    acc_ref[...] = jnp.zeros_like(acc_ref)

    def scores(start):
        return _mm_nt(k_ref[pl.ds(start, tk), :], qs_ref[...])

    def update(s_t, start):
        m_prev = m_ref[...]
        m_new = jnp.maximum(m_prev, jnp.max(s_t, axis=0, keepdims=True))
        alpha = jnp.exp2(m_prev - m_new)
        p_t = jnp.exp2(s_t - m_new).astype(BF16)
        acc_ref[...] = alpha * acc_ref[...] + jnp.dot(vt_ref[:, pl.ds(start, tk)], p_t,
                                                      preferred_element_type=F32)
        m_ref[...] = m_new

    assert tq == 2 * tk

    def block_start(n):
        return pl.multiple_of(n * tk, tk)

    s_ref[...] = scores(block_start(0))

    def body(i, carry):
        s_even = s_ref[...]
        s_odd = scores(block_start(2 * i + 1))
        update(s_even, block_start(2 * i))
        s_ref[...] = scores(block_start(2 * i + 2))
        update(s_odd, block_start(2 * i + 1))
        return carry

    lax.fori_loop(0, qi, body, 0)

    def causal(s_t, u):
        key_chunk = (lax.broadcasted_iota(jnp.int32, s_t.shape, 0) + u * tk) >> 6
        q_chunk = (lax.broadcasted_iota(jnp.int32, s_t.shape, 1) & (tq - 1)) >> 6
        return jnp.where(key_chunk <= q_chunk, s_t, NEG_BIG)

    s_odd = scores(block_start(2 * qi + 1))
    update(causal(s_ref[...], 0), block_start(2 * qi))
    update(causal(s_odd, 1), block_start(2 * qi + 1))

    lp = lamp_ref[...]
    lam = (jnp.exp(jnp.sum(lp[0:1] * lp[1:2], axis=-1, keepdims=True))
           - jnp.exp(jnp.sum(lp[2:3] * lp[3:4], axis=-1, keepdims=True)) + lam_init)
    acc = acc_ref[...]
    o_all = acc[0:B_V_DIM] * (1.0 / acc[B_V_DIM:B_V_DIM + 1])
    o = (o_all[:, :tq] - lam * o_all[:, tq:]).T
    y = _rms(o, bnorm_ref[...]) * (1.0 - lam_init) * _silu(z_ref[...].astype(F32))
    o_ref[...] = y.astype(o_ref.dtype)


def dattn_heads(q_rot, k_rot, proj, lam_params, b_norm, *, batch, seq, tq, tk, lam_init):
    t = q_rot.shape[0]
    tq = min(tq, seq)
    tk = min(tk, tq)
    nq = seq // tq
    vcol = (4 * A_WIDTH + 2 * B_WIDTH) // LANES
    zcol = (4 * A_WIDTH + 3 * B_WIDTH) // LANES
    return pl.pallas_call(
        functools.partial(_dattn_kernel, tq=tq, tk=tk, seq=seq, lam_init=lam_init),
        grid=(batch, B_HEADS, nq),
        in_specs=[pl.BlockSpec((tq, LANES), lambda b, h, i: (b * nq + i, h)),
                  pl.BlockSpec((seq, LANES), lambda b, h, i: (b, h)),
                  pl.BlockSpec((seq, LANES), lambda b, h, i: (b, vcol + h)),
                  pl.BlockSpec((tq, LANES), lambda b, h, i: (b * nq + i, zcol + h)),
                  pl.BlockSpec((4, B_QK_DIM), lambda b, h, i: (0, 0)),
                  pl.BlockSpec((1, LANES), lambda b, h, i: (0, 0))],
        out_specs=pl.BlockSpec((tq, LANES), lambda b, h, i: (b * nq + i, h)),
        out_shape=jax.ShapeDtypeStruct((t, B_WIDTH), BF16),
        scratch_shapes=[pltpu.VMEM((2 * tq, LANES), BF16),
                        pltpu.VMEM((B_V_DIM + DATTN_ONES_ROWS, seq), BF16),
                        pltpu.VMEM((1, 2 * tq), F32),
                        pltpu.VMEM((B_V_DIM + DATTN_ONES_ROWS, 2 * tq), F32),
                        pltpu.VMEM((tk, 2 * tq), F32)],
        compiler_params=_cparams(("parallel", "parallel", "arbitrary")),
        name="dattn_heads",
    )(q_rot, k_rot, proj, proj, lam_params, b_norm)


def _ab_out_kernel(ya_ref, yb_ref, w_ref, g_ref, x_ref, o_ref, wb_ref):
    @pl.when(pl.program_id(0) == 0)
    def _():
        _cast_rows(w_ref, wb_ref)

    y = (jnp.dot(ya_ref[...], wb_ref[0:A_WIDTH, :], preferred_element_type=F32)
         + jnp.dot(yb_ref[...], wb_ref[A_WIDTH:, :], preferred_element_type=F32))
    o_ref[...] = x_ref[...] + _rms(y, g_ref[...])


def ab_out_proj(y_a, y_b, w, gain, x, *, tm):
    t, d = x.shape
    tm = min(tm, t)
    return pl.pallas_call(
        _ab_out_kernel, grid=(t // tm,),
        in_specs=[pl.BlockSpec((tm, A_WIDTH), lambda i: (i, 0)),
                  pl.BlockSpec((tm, B_WIDTH), lambda i: (i, 0)),
                  pl.BlockSpec((A_WIDTH + B_WIDTH, d), lambda i: (0, 0), pipeline_mode=pl.Buffered(1)),
                  pl.BlockSpec((1, d), lambda i: (0, 0)),
                  pl.BlockSpec((tm, d), lambda i: (i, 0))],
        out_specs=pl.BlockSpec((tm, d), lambda i: (i, 0)),
        out_shape=jax.ShapeDtypeStruct((t, d), F32),
        scratch_shapes=[pltpu.VMEM((A_WIDTH + B_WIDTH, d), BF16)],
        compiler_params=_cparams(("arbitrary",)), name="ab_out_proj",
    )(y_a, y_b, w, gain.reshape(1, d), x)


def _conv_out_kernel(u_ref, b_ref, c_ref, z_ref, cw_ref, w_ref, g_ref, x_ref, o_ref, tail_ref, wb_ref,
                     *, tm, tiles_per_seq):
    @pl.when(pl.program_id(0) == 0)
    def _():
        _cast_rows(w_ref, wb_ref)

    @pl.when(pl.program_id(0) % tiles_per_seq == 0)
    def _():
        tail_ref[...] = jnp.zeros_like(tail_ref)

    width = u_ref.shape[1]
    y = None
    for c0 in range(0, width, CONV_OUT_CHUNK):
        cs = slice(c0, c0 + CONV_OUT_CHUNK)
        cu = c_ref[:, cs].astype(F32) * u_ref[:, cs].astype(F32)
        tail = tail_ref[:, cs]
        cw = cw_ref[:, cs]
        conv = cu * cw[C_CONV - 1:C_CONV]
        for j in range(C_CONV - 1):
            conv = conv + _shifted_rows(tail, cu, C_CONV - 1 - j) * cw[j:j + 1]
        tail_ref[:, cs] = cu[tm - SUBLANES:]
        gated = b_ref[:, cs].astype(F32) * conv * _silu(z_ref[:, cs].astype(F32))
        part = jnp.dot(gated.astype(BF16), wb_ref[cs, :], preferred_element_type=F32)
        y = part if y is None else y + part
    o_ref[...] = x_ref[...] + _rms(y, g_ref[...])


def conv_out_proj(proj, conv_w, w_out, gain, x, *, seq, tm):
    t, d = x.shape
    tm = min(tm, seq)

    def col(c):
        return pl.BlockSpec((tm, d), lambda i: (i, c))

    return pl.pallas_call(
        functools.partial(_conv_out_kernel, tm=tm, tiles_per_seq=seq // tm),
        grid=(t // tm,),
        in_specs=[col(0), col(1), col(2), col(3),
                  pl.BlockSpec((C_CONV, d), lambda i: (0, 0)),
                  pl.BlockSpec((d, d), lambda i: (0, 0), pipeline_mode=pl.Buffered(1)),
                  pl.BlockSpec((1, d), lambda i: (0, 0)),
                  pl.BlockSpec((tm, d), lambda i: (i, 0))],
        out_specs=pl.BlockSpec((tm, d), lambda i: (i, 0)),
        out_shape=jax.ShapeDtypeStruct((t, d), F32),
        scratch_shapes=[pltpu.VMEM((SUBLANES, d), F32), pltpu.VMEM((d, d), BF16)],
        compiler_params=_cparams(("arbitrary",)), name="conv_out_proj",
    )(proj, proj, proj, proj, conv_w, w_out, gain.reshape(1, d), x)


def _xattn_kernel(x_ref, gpre_ref, wq32_ref, k_ref, v_ref, wo32_ref, gpost_ref, o_ref, wq_ref, wo_ref):
    @pl.when(pl.program_id(0) == 0)
    def _():
        _cast_rows(wq32_ref, wq_ref)
        _cast_rows(wo32_ref, wo_ref)

    x = x_ref[...]
    h = _rms(x, gpre_ref[...]).astype(BF16)
    qz = jnp.dot(h, wq_ref[...], preferred_element_type=F32)
    q = qz[:, :M_WIDTH] * (M_HEAD_DIM ** -0.5)
    z = qz[:, M_WIDTH:]
    heads = []
    for hd in range(M_HEADS):
        sl = slice(hd * M_HEAD_DIM, (hd + 1) * M_HEAD_DIM)
        s = _mm_nt(q[:, sl], k_ref[:, sl])
        p = jnp.exp(s - jnp.max(s, axis=-1, keepdims=True))
        p = p * (1.0 / jnp.sum(p, axis=-1, keepdims=True))
        heads.append(_mm(p, v_ref[:, sl]))
    o = jnp.concatenate(heads, axis=-1) * _silu(z)
    y = jnp.dot(o.astype(BF16), wo_ref[...], preferred_element_type=F32)
    o_ref[...] = x + _rms(y, gpost_ref[...])


def xattn_sublayer(x, g_pre, w_q, kv, w_o, g_post, *, seq, n_mem, tm):
    t, d = x.shape
    tm = min(tm, seq)
    spb = seq // tm
    return pl.pallas_call(
        _xattn_kernel, grid=(t // tm,),
        in_specs=[pl.BlockSpec((tm, d), lambda i: (i, 0)),
                  pl.BlockSpec((1, d), lambda i: (0, 0)),
                  pl.BlockSpec((d, 2 * M_WIDTH), lambda i: (0, 0), pipeline_mode=pl.Buffered(1)),
                  pl.BlockSpec((n_mem, M_WIDTH), lambda i: (i // spb, 0)),
                  pl.BlockSpec((n_mem, M_WIDTH), lambda i: (i // spb, 1)),
                  pl.BlockSpec((M_WIDTH, d), lambda i: (0, 0), pipeline_mode=pl.Buffered(1)),
                  pl.BlockSpec((1, d), lambda i: (0, 0))],
        out_specs=pl.BlockSpec((tm, d), lambda i: (i, 0)),
        out_shape=jax.ShapeDtypeStruct((t, d), F32),
        scratch_shapes=[pltpu.VMEM((d, 2 * M_WIDTH), BF16), pltpu.VMEM((M_WIDTH, d), BF16)],
        compiler_params=_cparams(("arbitrary",)), name="xattn_sublayer",
    )(x, g_pre.reshape(1, d), w_q, kv, kv, w_o, g_post.reshape(1, d))


def _lambda_init(layer):
    return 0.8 - 0.6 * math.exp(-0.3 * layer)


def _rope_lane_tables(seq):
    half = B_QK_DIM // 2
    inv_freq = ROPE_THETA ** (-jnp.arange(0, B_QK_DIM, 2, dtype=F32) / B_QK_DIM)
    ang = jnp.arange(seq, dtype=F32)[:, None] * inv_freq[None, :]
    cos, sin = jnp.cos(ang), jnp.sin(ang)
    assert cos.shape == (seq, half)
    return jnp.tile(cos, (1, 4)), jnp.concatenate([-sin, sin, -sin, sin], axis=-1)


def _lane_row(values, offset):
    return jnp.zeros((1, LANES), F32).at[0, offset:offset + values.shape[0]].set(values.astype(F32))


def kernel(x, mem, norm_pre, norm_post, xattn_norm_pre, xattn_norm_post, mem_norm,
           ab_w_in, a_conv, a_A_log, a_dt_bias, a_out_norm, b_lambda, b_out_norm, ab_w_out,
           c_w_in, c_conv, c_w_out, m_w_q, m_w_kv, m_w_o):
    batch, seq, d = x.shape
    n_mem = mem.shape[1]
    depth = norm_pre.shape[0]
    cos_t, sin_t = _rope_lane_tables(seq)
    xf = x.reshape(batch * seq, d)
    memf = mem.reshape(batch * n_mem, d)
    gate_lo = 4 * A_WIDTH
    gate_hi = gate_lo + 2 * A_HEADS

    for layer in range(depth):
        if layer % 2 == 0:
            e = layer // 2
            w_in = ab_w_in[e]
            w_main = jnp.concatenate([w_in[:, :gate_lo], w_in[:, gate_hi:]], axis=1).astype(BF16)
            w_gates = jnp.pad(w_in[:, gate_lo:gate_hi], ((0, 0), (0, LANES - 2 * A_HEADS))).astype(BF16)
            proj, gates = norm_matmul(xf, norm_pre[layer], w_main, tm=1024, tn=1024, w_gates=w_gates)
            y_a = gdn_heads(proj, gates, a_conv[e], _lane_row(a_A_log[e], A_HEADS),
                            _lane_row(a_dt_bias[e], A_HEADS), a_out_norm[e].reshape(1, A_HEAD_DIM),
                            batch=batch, seq=seq, ts=256, hb=8)
            q_rot, k_rot = rope_qk(proj, cos_t, sin_t, seq=seq, tm=512)
            y_b = dattn_heads(q_rot, k_rot, proj, b_lambda[e], b_out_norm[e].reshape(1, B_V_DIM),
                              batch=batch, seq=seq, tq=512, tk=256, lam_init=_lambda_init(layer))
            xf = ab_out_proj(y_a, y_b, ab_w_out[e], norm_post[layer], xf, tm=512)
        else:
            o = layer // 2
            proj = norm_matmul(xf, norm_pre[layer], c_w_in[o], tm=1024, tn=1024)
            xf = conv_out_proj(proj, c_conv[o], c_w_out[o], norm_post[layer], xf, seq=seq, tm=256)
        kv = norm_matmul(memf, mem_norm[layer], m_w_kv[layer], tm=512, tn=2 * M_WIDTH)
        xf = xattn_sublayer(xf, xattn_norm_pre[layer], m_w_q[layer], kv,
                            m_w_o[layer], xattn_norm_post[layer],
                            seq=seq, n_mem=n_mem, tm=512)
    return xf.reshape(batch, seq, d)
```

```python
import functools
import math

import jax
import jax.numpy as jnp
from jax import lax
from jax.experimental import pallas as pl
from jax.experimental.pallas import tpu as pltpu

F32 = jnp.float32
BF16 = jnp.bfloat16

EPS = 1e-6
CHUNK = 64
PACK = 256
A_HEADS = 8
A_HEAD_DIM = 128
A_WIDTH = A_HEADS * A_HEAD_DIM
A_CONV = 4
B_HEADS = 8
B_QK_DIM = 64
B_V_DIM = 128
B_WIDTH = B_HEADS * B_V_DIM
ROPE_THETA = 10000.0
C_CONV = 3
M_HEADS = 4
M_HEAD_DIM = 128
M_WIDTH = M_HEADS * M_HEAD_DIM
LANES = 128
SUBLANES = 8
NEG_BIG = -1e30
DATTN_ONES_ROWS = 16
CAST_ROWS = 256
CONV_OUT_CHUNK = 512
VMEM_LIMIT = 56 * 1024 * 1024


def _cparams(sem):
    return pltpu.CompilerParams(dimension_semantics=sem, vmem_limit_bytes=VMEM_LIMIT)


def _mm(a, b):
    return jnp.dot(a.astype(BF16), b.astype(BF16), preferred_element_type=F32)


def _mm_nt(a, b):
    return lax.dot_general(a.astype(BF16), b.astype(BF16), (((1,), (1,)), ((), ())),
                           preferred_element_type=F32)


def _mm_tn(a, b):
    return lax.dot_general(a.astype(BF16), b.astype(BF16), (((0,), (0,)), ((), ())),
                           preferred_element_type=F32)


def _rms(x, gain):
    return x * lax.rsqrt(jnp.mean(x * x, axis=-1, keepdims=True) + EPS) * gain


def _sigmoid(x):
    return 0.5 * jnp.tanh(0.5 * x) + 0.5


def _silu(x):
    h = 0.5 * x
    return h + h * jnp.tanh(h)


def _softplus(x):
    return jnp.maximum(x, 0.0) + jnp.log1p(jnp.exp(-jnp.abs(x)))


def _cast_rows(src_ref, dst_ref):
    def body(i, carry):
        r = pl.multiple_of(i * CAST_ROWS, CAST_ROWS)
        dst_ref[pl.ds(r, CAST_ROWS), :] = src_ref[pl.ds(r, CAST_ROWS), :].astype(dst_ref.dtype)
        return carry

    lax.fori_loop(0, src_ref.shape[0] // CAST_ROWS, body, 0)


def _shifted_rows(tail, x, shift):
    ext = jnp.concatenate([tail, x], axis=0)
    return pltpu.roll(ext, shift, axis=0)[SUBLANES:]


def _norm_matmul_kernel(x_ref, g_ref, w_ref, o_ref, h_ref):
    @pl.when(pl.program_id(1) == 0)
    def _():
        h_ref[...] = _rms(x_ref[...], g_ref[...]).astype(BF16)

    o_ref[...] = jnp.dot(h_ref[...], w_ref[...].astype(BF16), preferred_element_type=F32).astype(o_ref.dtype)


def _norm_matmul_gates_kernel(x_ref, g_ref, w_ref, wg_ref, o_ref, og_ref, h_ref):
    @pl.when(pl.program_id(1) == 0)
    def _():
        h_ref[...] = _rms(x_ref[...], g_ref[...]).astype(BF16)
        og_ref[...] = jnp.dot(h_ref[...], wg_ref[...], preferred_element_type=F32)

    o_ref[...] = jnp.dot(h_ref[...], w_ref[...], preferred_element_type=F32).astype(o_ref.dtype)


def norm_matmul(x, gain, w, *, tm, tn, w_gates=None, w_index=None):
    t, d = x.shape
    n = w.shape[-1]
    tm = min(tm, t)
    grid = (t // tm, n // tn)
    x_spec = pl.BlockSpec((tm, d), lambda i, j: (i, 0))
    g_spec = pl.BlockSpec((1, d), lambda i, j: (0, 0))
    if w_index is None:
        w_spec = pl.BlockSpec((d, tn), lambda i, j: (0, j))
    else:
        w_spec = pl.BlockSpec((None, d, tn), lambda i, j: (w_index, 0, j))
    o_spec = pl.BlockSpec((tm, tn), lambda i, j: (i, j))
    scratch = [pltpu.VMEM((tm, d), BF16)]
    if w_gates is None:
        return pl.pallas_call(
            _norm_matmul_kernel, grid=grid,
            in_specs=[x_spec, g_spec, w_spec], out_specs=o_spec,
            out_shape=jax.ShapeDtypeStruct((t, n), BF16),
            scratch_shapes=scratch, compiler_params=_cparams(("parallel", "arbitrary")),
            name="norm_matmul",
        )(x, gain.reshape(1, d), w)
    return pl.pallas_call(
        _norm_matmul_gates_kernel, grid=grid,
        in_specs=[x_spec, g_spec, w_spec, pl.BlockSpec((d, LANES), lambda i, j: (0, 0))],
        out_specs=[o_spec, pl.BlockSpec((tm, LANES), lambda i, j: (i, 0))],
        out_shape=[jax.ShapeDtypeStruct((t, n), BF16), jax.ShapeDtypeStruct((t, LANES), F32)],
        scratch_shapes=scratch, compiler_params=_cparams(("parallel", "arbitrary")),
        name="norm_matmul_gates",
    )(x, gain.reshape(1, d), w, w_gates)


def _rope_kernel(q_ref, k_ref, cos_ref, sin_ref, qo_ref, ko_ref):
    cos = cos_ref[...]
    sin = sin_ref[...]
    lane = lax.broadcasted_iota(jnp.int32, cos.shape, 1)
    first_half = (lane & 32) == 0

    def rot(ref, out_ref, scale):
        for h in range(B_HEADS):
            sl = slice(h * LANES, (h + 1) * LANES)
            t = ref[:, sl].astype(F32)
            partner = jnp.where(first_half, pltpu.roll(t, LANES - 32, axis=1), pltpu.roll(t, 32, axis=1))
            out_ref[:, sl] = ((t * cos + partner * sin) * scale).astype(out_ref.dtype)

    rot(q_ref, qo_ref, B_QK_DIM ** -0.5 * math.log2(math.e))
    rot(k_ref, ko_ref, 1.0)


def rope_qk(proj, cos_t, sin_t, *, seq, tm):
    t = proj.shape[0]
    tm = min(tm, seq)
    spb = seq // tm
    qcol = 4 * A_WIDTH // B_WIDTH
    return pl.pallas_call(
        _rope_kernel, grid=(t // tm,),
        in_specs=[pl.BlockSpec((tm, B_WIDTH), lambda i: (i, qcol)),
                  pl.BlockSpec((tm, B_WIDTH), lambda i: (i, qcol + 1)),
                  pl.BlockSpec((tm, LANES), lambda i: (i % spb, 0)),
                  pl.BlockSpec((tm, LANES), lambda i: (i % spb, 0))],
        out_specs=[pl.BlockSpec((tm, B_WIDTH), lambda i: (i, 0)),
                   pl.BlockSpec((tm, B_WIDTH), lambda i: (i, 0))],
        out_shape=[jax.ShapeDtypeStruct((t, B_WIDTH), BF16)] * 2,
        compiler_params=_cparams(("parallel",)), name="rope_qk",
    )(proj, proj, cos_t, sin_t)


def _gdn_kernel(q_ref, k_ref, v_ref, z_ref, bg_ref, cwq_ref, cwk_ref, cwv_ref,
                arow_ref, dtrow_ref, anorm_ref, o_ref,
                state_ref, tq_ref, tk_ref, tv_ref, *, ts, hb):
    hg = pl.program_id(1)

    @pl.when(pl.program_id(2) == 0)
    def _():
        state_ref[...] = jnp.zeros_like(state_ref)
        tq_ref[...] = jnp.zeros_like(tq_ref)
        tk_ref[...] = jnp.zeros_like(tk_ref)
        tv_ref[...] = jnp.zeros_like(tv_ref)

    def conv_silu(x_ref, w_ref, tail_ref):
        x = x_ref[...].astype(F32)
        tail = tail_ref[...]
        w = w_ref[...]
        acc = x * w[A_CONV - 1:A_CONV]
        for j in range(A_CONV - 1):
            acc = acc + _shifted_rows(tail, x, A_CONV - 1 - j) * w[j:j + 1]
        tail_ref[...] = x[ts - SUBLANES:]
        return _silu(acc)

    q_all = conv_silu(q_ref, cwq_ref, tq_ref)
    k_all = conv_silu(k_ref, cwk_ref, tk_ref)
    v_all = conv_silu(v_ref, cwv_ref, tv_ref)

    bg = bg_ref[...]
    beta_all = _sigmoid(bg)
    dec = -jnp.exp(arow_ref[...]) * _softplus(bg + dtrow_ref[...])
    row_in_chunk = lax.broadcasted_iota(jnp.int32, (ts, 1), 0) & (CHUNK - 1)
    step = 1
    while step < CHUNK:
        dec = dec + jnp.where(row_in_chunk >= step, pltpu.roll(dec, step, axis=0), 0.0)
        step *= 2
    lane = lax.broadcasted_iota(jnp.int32, (ts, LANES), 1)

    ii = lax.broadcasted_iota(jnp.int32, (PACK, PACK), 0)
    jj = lax.broadcasted_iota(jnp.int32, (PACK, PACK), 1)
    same64 = (ii >> 6) == (jj >> 6)
    tril = same64 & (ii >= jj)
    strict = same64 & (ii > jj)
    same16 = (ii >> 4) == (jj >> 4)
    same32 = (ii >> 5) == (jj >> 5)
    eye = jnp.where(ii == jj, 1.0, 0.0)

    heads = range(hb)
    packs = range(ts // PACK)
    sls = [slice(j * A_HEAD_DIM, (j + 1) * A_HEAD_DIM) for j in heads]
    qh, kh, d, kb, rhs, qd = [], [], [], [], [], []
    for j in heads:
        head = hg * hb + j
        q_j = q_all[:, sls[j]]
        k_j = k_all[:, sls[j]]
        q_j = q_j * (lax.rsqrt(jnp.sum(q_j * q_j, axis=-1, keepdims=True) + EPS) * (A_HEAD_DIM ** -0.5))
        k_j = k_j * lax.rsqrt(jnp.sum(k_j * k_j, axis=-1, keepdims=True) + EPS)
        beta = jnp.sum(jnp.where(lane == head, beta_all, 0.0), axis=-1, keepdims=True)
        d_j = jnp.sum(jnp.where(lane == head + A_HEADS, dec, 0.0), axis=-1, keepdims=True)
        exp_d = jnp.exp(d_j)
        kb_j = k_j * beta
        qh.append(q_j)
        kh.append(k_j)
        d.append(d_j)
        kb.append(kb_j)
        rhs.append(jnp.concatenate([v_all[:, sls[j]] * beta, kb_j * exp_d], axis=-1))
        qd.append(q_j * exp_d)

    units = [(j, slice(p * PACK, (p + 1) * PACK)) for p in packs for j in heads]

    def decay_matrix(j, rows):
        dp = d[j][rows]
        drow = jnp.broadcast_to(dp, (PACK, LANES)).T[0:1, :]
        return jnp.where(tril, jnp.exp(jnp.where(tril, dp - drow, 0.0)), 0.0)

    gamma = [decay_matrix(j, rows) for j, rows in units]
    nmat = [jnp.where(strict, _mm_nt(kb[j][rows], kh[j][rows]) * g, 0.0) for (j, rows), g in zip(units, gamma)]
    qk = [_mm_nt(qh[j][rows], kh[j][rows]) * g for (j, rows), g in zip(units, gamma)]
    p1 = [jnp.where(same16, -n, 0.0) for n in nmat]
    inv = [eye + a for a in p1]
    p2 = [_mm(a, a) for a in p1]
    inv = [i + _mm(i, a) for i, a in zip(inv, p2)]
    p4 = [_mm(a, a) for a in p2]
    inv = [i + _mm(i, a) for i, a in zip(inv, p4)]
    p8 = [_mm(a, a) for a in p4]
    inv = [i + _mm(i, a) for i, a in zip(inv, p8)]
    not16 = jnp.logical_not(same16)
    t1 = [_mm(jnp.where(same32 & not16, n, 0.0), i) for n, i in zip(nmat, inv)]
    inv = [i - _mm(i, a) for i, a in zip(inv, t1)]
    t2 = [_mm(jnp.where(same32, 0.0, n), i) for n, i in zip(nmat, inv)]
    inv = [i - _mm(i, a) for i, a in zip(inv, t2)]
    sol = [_mm(i, rhs[j][rows]) for (j, rows), i in zip(units, inv)]

    state = [state_ref[j] for j in heads]
    outs = [[] for _ in heads]
    for p in packs:
        v_new = [[] for _ in heads]
        o_inter = [[] for _ in heads]
        for c in range(PACK // CHUNK):
            cr = slice(c * CHUNK, (c + 1) * CHUNK)
            gr = slice(p * PACK + c * CHUNK, p * PACK + (c + 1) * CHUNK)
            for j in heads:
                s_uw = sol[p * hb + j]
                d_c = d[j][gr]
                d_last = d_c[CHUNK - 1:CHUNK]
                vn = s_uw[cr, :A_HEAD_DIM] - _mm(s_uw[cr, A_HEAD_DIM:], state[j])
                o_inter[j].append(_mm(qd[j][gr], state[j]))
                kd = kh[j][gr] * jnp.exp(d_last - d_c)
                state[j] = state[j] * jnp.exp(d_last) + _mm_tn(kd, vn)
                v_new[j].append(vn)
        for j in heads:
            outs[j].append(jnp.concatenate(o_inter[j], axis=0)
                           + _mm(qk[p * hb + j], jnp.concatenate(v_new[j], axis=0)))
    for j in heads:
        state_ref[j] = state[j]
        o = jnp.concatenate(outs[j], axis=0) if len(outs[j]) > 1 else outs[j][0]
        y = _rms(o, anorm_ref[...]) * _silu(z_ref[:, sls[j]].astype(F32))
        o_ref[:, sls[j]] = y.astype(o_ref.dtype)


def gdn_heads(proj, gates, conv_w, a_row, dt_row, a_norm, *, batch, seq, ts, hb):
    t = proj.shape[0]
    ts = min(ts, seq)
    nst = seq // ts
    ngrp = A_HEADS // hb
    wblk = hb * A_HEAD_DIM

    def col(off):
        return pl.BlockSpec((ts, wblk), lambda b, g, s: (b * nst + s, off * ngrp + g))

    def cw(off):
        return pl.BlockSpec((A_CONV, wblk), lambda b, g, s: (0, off * ngrp + g))

    row = pl.BlockSpec((1, LANES), lambda b, g, s: (0, 0))
    return pl.pallas_call(
        functools.partial(_gdn_kernel, ts=ts, hb=hb),
        grid=(batch, ngrp, nst),
        in_specs=[col(0), col(1), col(2), col(3),
                  pl.BlockSpec((ts, LANES), lambda b, g, s: (b * nst + s, 0)),
                  cw(0), cw(1), cw(2), row, row, row],
        out_specs=pl.BlockSpec((ts, wblk), lambda b, g, s: (b * nst + s, g)),
        out_shape=jax.ShapeDtypeStruct((t, A_WIDTH), BF16),
        scratch_shapes=[pltpu.VMEM((hb, A_HEAD_DIM, A_HEAD_DIM), F32),
                        pltpu.VMEM((SUBLANES, wblk), F32),
                        pltpu.VMEM((SUBLANES, wblk), F32),
                        pltpu.VMEM((SUBLANES, wblk), F32)],
        compiler_params=_cparams(("parallel", "parallel", "arbitrary")),
        name="gdn_heads",
    )(proj, proj, proj, proj, gates, conv_w, conv_w, conv_w, a_row, dt_row, a_norm)


def _dattn_kernel(q_ref, k_ref, v_ref, z_ref, lamp_ref, bnorm_ref, o_ref,
                  qs_ref, vt_ref, m_ref, acc_ref, s_ref, *, tq, tk, seq, lam_init):
    qi = pl.program_id(2)

    @pl.when(qi == 0)
    def _():
        for c in range(seq // tk):
            cs = slice(c * tk, (c + 1) * tk)
            vt_ref[0:B_V_DIM, cs] = v_ref[cs, :].astype(F32).T.astype(BF16)
        vt_ref[B_V_DIM:, :] = jnp.ones((DATTN_ONES_ROWS, seq), BF16)

    q = q_ref[...]
    lane = lax.broadcasted_iota(jnp.int32, q.shape, 1)
    zero = jnp.zeros_like(q)
    qs_ref[0:tq, :] = jnp.where(lane < B_QK_DIM, q, zero)
    qs_ref[tq:, :] = jnp.where(lane < B_QK_DIM, zero, q)
    m_ref[...] = jnp.full_like(m_ref, NEG_BIG)
    acc_ref[...] = jnp.zeros_like(acc_ref)

    def scores(start):
        return _mm_nt(k_ref[pl.ds(start, tk), :], qs_ref[...])

    def update(s_t, start):
        m_prev = m_ref[...]
        m_new = jnp.maximum(m_prev, jnp.max(s_t, axis=0, keepdims=True))
        alpha = jnp.exp2(m_prev - m_new)
        p_t = jnp.exp2(s_t - m_new).astype(BF16)
        acc_ref[...] = alpha * acc_ref[...] + jnp.dot(vt_ref[:, pl.ds(start, tk)], p_t,
                                                      preferred_element_type=F32)
        m_ref[...] = m_new

    assert tq == 2 * tk

    def block_start(n):
        return pl.multiple_of(n * tk, tk)

    s_ref[...] = scores(block_start(0))

    def body(i, carry):
        s_even = s_ref[...]
        s_odd = scores(block_start(2 * i + 1))
        update(s_even, block_start(2 * i))
        s_ref[...] = scores(block_start(2 * i + 2))
        update(s_odd, block_start(2 * i + 1))
        return carry

    lax.fori_loop(0, qi, body, 0)

    def causal(s_t, u):
        key_chunk = (lax.broadcasted_iota(jnp.int32, s_t.shape, 0) + u * tk) >> 6
        q_chunk = (lax.broadcasted_iota(jnp.int32, s_t.shape, 1) & (tq - 1)) >> 6
        return jnp.where(key_chunk <= q_chunk, s_t, NEG_BIG)

    s_odd = scores(block_start(2 * qi + 1))
    update(causal(s_ref[...], 0), block_start(2 * qi))
    update(causal(s_odd, 1), block_start(2 * qi + 1))

    lp = lamp_ref[...]
    lam = (jnp.exp(jnp.sum(lp[0:1] * lp[1:2], axis=-1, keepdims=True))
           - jnp.exp(jnp.sum(lp[2:3] * lp[3:4], axis=-1, keepdims=True)) + lam_init)
    acc = acc_ref[...]
    o_all = acc[0:B_V_DIM] * (1.0 / acc[B_V_DIM:B_V_DIM + 1])
    o = (o_all[:, :tq] - lam * o_all[:, tq:]).T
    y = _rms(o, bnorm_ref[...]) * (1.0 - lam_init) * _silu(z_ref[...].astype(F32))
    o_ref[...] = y.astype(o_ref.dtype)


def dattn_heads(q_rot, k_rot, proj, lam_params, b_norm, *, batch, seq, tq, tk, lam_init):
    t = q_rot.shape[0]
    tq = min(tq, seq)
    tk = min(tk, tq)
    nq = seq // tq
    vcol = (4 * A_WIDTH + 2 * B_WIDTH) // LANES
    zcol = (4 * A_WIDTH + 3 * B_WIDTH) // LANES
    return pl.pallas_call(
        functools.partial(_dattn_kernel, tq=tq, tk=tk, seq=seq, lam_init=lam_init),
        grid=(batch, B_HEADS, nq),
        in_specs=[pl.BlockSpec((tq, LANES), lambda b, h, i: (b * nq + i, h)),
                  pl.BlockSpec((seq, LANES), lambda b, h, i: (b, h)),
                  pl.BlockSpec((seq, LANES), lambda b, h, i: (b, vcol + h)),
                  pl.BlockSpec((tq, LANES), lambda b, h, i: (b * nq + i, zcol + h)),
                  pl.BlockSpec((4, B_QK_DIM), lambda b, h, i: (0, 0)),
                  pl.BlockSpec((1, LANES), lambda b, h, i: (0, 0))],
        out_specs=pl.BlockSpec((tq, LANES), lambda b, h, i: (b * nq + i, h)),
        out_shape=jax.ShapeDtypeStruct((t, B_WIDTH), BF16),
        scratch_shapes=[pltpu.VMEM((2 * tq, LANES), BF16),
                        pltpu.VMEM((B_V_DIM + DATTN_ONES_ROWS, seq), BF16),
                        pltpu.VMEM((1, 2 * tq), F32),
                        pltpu.VMEM((B_V_DIM + DATTN_ONES_ROWS, 2 * tq), F32),
                        pltpu.VMEM((tk, 2 * tq), F32)],
        compiler_params=_cparams(("parallel", "parallel", "arbitrary")),
        name="dattn_heads",
    )(q_rot, k_rot, proj, proj, lam_params, b_norm)


def _ab_out_kernel(ya_ref, yb_ref, w_ref, g_ref, x_ref, o_ref, wb_ref):
    @pl.when(pl.program_id(0) == 0)
    def _():
        _cast_rows(w_ref, wb_ref)

    y = (jnp.dot(ya_ref[...], wb_ref[0:A_WIDTH, :], preferred_element_type=F32)
         + jnp.dot(yb_ref[...], wb_ref[A_WIDTH:, :], preferred_element_type=F32))
    o_ref[...] = x_ref[...] + _rms(y, g_ref[...])


def ab_out_proj(y_a, y_b, w, w_index, gain, x, *, tm):
    t, d = x.shape
    tm = min(tm, t)
    return pl.pallas_call(
        _ab_out_kernel, grid=(t // tm,),
        in_specs=[pl.BlockSpec((tm, A_WIDTH), lambda i: (i, 0)),
                  pl.BlockSpec((tm, B_WIDTH), lambda i: (i, 0)),
                  pl.BlockSpec((None, A_WIDTH + B_WIDTH, d), lambda i: (w_index, 0, 0),
                               pipeline_mode=pl.Buffered(1)),
                  pl.BlockSpec((1, d), lambda i: (0, 0)),
                  pl.BlockSpec((tm, d), lambda i: (i, 0))],
        out_specs=pl.BlockSpec((tm, d), lambda i: (i, 0)),
        out_shape=jax.ShapeDtypeStruct((t, d), F32),
        scratch_shapes=[pltpu.VMEM((A_WIDTH + B_WIDTH, d), BF16)],
        compiler_params=_cparams(("arbitrary",)), name="ab_out_proj",
    )(y_a, y_b, w, gain.reshape(1, d), x)


def _conv_out_kernel(u_ref, b_ref, c_ref, z_ref, cw_ref, w_ref, g_ref, x_ref, o_ref, tail_ref, wb_ref,
                     *, tm, tiles_per_seq):
    @pl.when(pl.program_id(0) == 0)
    def _():
        _cast_rows(w_ref, wb_ref)

    @pl.when(pl.program_id(0) % tiles_per_seq == 0)
    def _():
        tail_ref[...] = jnp.zeros_like(tail_ref)

    width = u_ref.shape[1]
    y = None
    for c0 in range(0, width, CONV_OUT_CHUNK):
        cs = slice(c0, c0 + CONV_OUT_CHUNK)
        cu = c_ref[:, cs].astype(F32) * u_ref[:, cs].astype(F32)
        tail = tail_ref[:, cs]
        cw = cw_ref[:, cs]
        conv = cu * cw[C_CONV - 1:C_CONV]
        for j in range(C_CONV - 1):
            conv = conv + _shifted_rows(tail, cu, C_CONV - 1 - j) * cw[j:j + 1]
        tail_ref[:, cs] = cu[tm - SUBLANES:]
        gated = b_ref[:, cs].astype(F32) * conv * _silu(z_ref[:, cs].astype(F32))
        part = jnp.dot(gated.astype(BF16), wb_ref[cs, :], preferred_element_type=F32)
        y = part if y is None else y + part
    o_ref[...] = x_ref[...] + _rms(y, g_ref[...])


def conv_out_proj(proj, conv_w, w_out, w_index, gain, x, *, seq, tm):
    t, d = x.shape
    tm = min(tm, seq)

    def col(c):
        return pl.BlockSpec((tm, d), lambda i: (i, c))

    return pl.pallas_call(
        functools.partial(_conv_out_kernel, tm=tm, tiles_per_seq=seq // tm),
        grid=(t // tm,),
        in_specs=[col(0), col(1), col(2), col(3),
                  pl.BlockSpec((C_CONV, d), lambda i: (0, 0)),
                  pl.BlockSpec((None, d, d), lambda i: (w_index, 0, 0), pipeline_mode=pl.Buffered(1)),
                  pl.BlockSpec((1, d), lambda i: (0, 0)),
                  pl.BlockSpec((tm, d), lambda i: (i, 0))],
        out_specs=pl.BlockSpec((tm, d), lambda i: (i, 0)),
        out_shape=jax.ShapeDtypeStruct((t, d), F32),
        scratch_shapes=[pltpu.VMEM((SUBLANES, d), F32), pltpu.VMEM((d, d), BF16)],
        compiler_params=_cparams(("arbitrary",)), name="conv_out_proj",
    )(proj, proj, proj, proj, conv_w, w_out, gain.reshape(1, d), x)


def _xattn_kernel(x_ref, gpre_ref, wq32_ref, k_ref, v_ref, wo32_ref, gpost_ref, o_ref, wq_ref, wo_ref):
    @pl.when(pl.program_id(0) == 0)
    def _():
        _cast_rows(wq32_ref, wq_ref)
        _cast_rows(wo32_ref, wo_ref)

    x = x_ref[...]
    h = _rms(x, gpre_ref[...]).astype(BF16)
    qz = jnp.dot(h, wq_ref[...], preferred_element_type=F32)
    q = qz[:, :M_WIDTH] * (M_HEAD_DIM ** -0.5)
    z = qz[:, M_WIDTH:]
    heads = []
    for hd in range(M_HEADS):
        sl = slice(hd * M_HEAD_DIM, (hd + 1) * M_HEAD_DIM)
        s = _mm_nt(q[:, sl], k_ref[:, sl])
        p = jnp.exp(s - jnp.max(s, axis=-1, keepdims=True))
        p = p * (1.0 / jnp.sum(p, axis=-1, keepdims=True))
        heads.append(_mm(p, v_ref[:, sl]))
    o = jnp.concatenate(heads, axis=-1) * _silu(z)
    y = jnp.dot(o.astype(BF16), wo_ref[...], preferred_element_type=F32)
    o_ref[...] = x + _rms(y, gpost_ref[...])


def xattn_sublayer(x, g_pre, w_q, kv, w_o, w_index, g_post, *, seq, n_mem, tm):
    t, d = x.shape
    tm = min(tm, seq)
    spb = seq // tm
    return pl.pallas_call(
        _xattn_kernel, grid=(t // tm,),
        in_specs=[pl.BlockSpec((tm, d), lambda i: (i, 0)),
                  pl.BlockSpec((1, d), lambda i: (0, 0)),
                  pl.BlockSpec((None, d, 2 * M_WIDTH), lambda i: (w_index, 0, 0),
                               pipeline_mode=pl.Buffered(1)),
                  pl.BlockSpec((n_mem, M_WIDTH), lambda i: (i // spb, 0)),
                  pl.BlockSpec((n_mem, M_WIDTH), lambda i: (i // spb, 1)),
                  pl.BlockSpec((None, M_WIDTH, d), lambda i: (w_index, 0, 0), pipeline_mode=pl.Buffered(1)),
                  pl.BlockSpec((1, d), lambda i: (0, 0))],
        out_specs=pl.BlockSpec((tm, d), lambda i: (i, 0)),
        out_shape=jax.ShapeDtypeStruct((t, d), F32),
        scratch_shapes=[pltpu.VMEM((d, 2 * M_WIDTH), BF16), pltpu.VMEM((M_WIDTH, d), BF16)],
        compiler_params=_cparams(("arbitrary",)), name="xattn_sublayer",
    )(x, g_pre.reshape(1, d), w_q, kv, kv, w_o, g_post.reshape(1, d))


def _lambda_init(layer):
    return 0.8 - 0.6 * math.exp(-0.3 * layer)


def _rope_lane_tables(seq):
    half = B_QK_DIM // 2
    inv_freq = ROPE_THETA ** (-jnp.arange(0, B_QK_DIM, 2, dtype=F32) / B_QK_DIM)
    ang = jnp.arange(seq, dtype=F32)[:, None] * inv_freq[None, :]
    cos, sin = jnp.cos(ang), jnp.sin(ang)
    assert cos.shape == (seq, half)
    return jnp.tile(cos, (1, 4)), jnp.concatenate([-sin, sin, -sin, sin], axis=-1)


def _lane_row(values, offset):
    return jnp.zeros((1, LANES), F32).at[0, offset:offset + values.shape[0]].set(values.astype(F32))


def kernel(x, mem, norm_pre, norm_post, xattn_norm_pre, xattn_norm_post, mem_norm,
           ab_w_in, a_conv, a_A_log, a_dt_bias, a_out_norm, b_lambda, b_out_norm, ab_w_out,
           c_w_in, c_conv, c_w_out, m_w_q, m_w_kv, m_w_o):
    batch, seq, d = x.shape
    n_mem = mem.shape[1]
    depth = norm_pre.shape[0]
    cos_t, sin_t = _rope_lane_tables(seq)
    xf = x.reshape(batch * seq, d)
    memf = mem.reshape(batch * n_mem, d)
    gate_lo = 4 * A_WIDTH
    gate_hi = gate_lo + 2 * A_HEADS

    for layer in range(depth):
        if layer % 2 == 0:
            e = layer // 2
            w_in = ab_w_in[e]
            w_main = jnp.concatenate([w_in[:, :gate_lo], w_in[:, gate_hi:]], axis=1).astype(BF16)
            w_gates = jnp.pad(w_in[:, gate_lo:gate_hi], ((0, 0), (0, LANES - 2 * A_HEADS))).astype(BF16)
            proj, gates = norm_matmul(xf, norm_pre[layer], w_main, tm=1024, tn=1024, w_gates=w_gates)
            y_a = gdn_heads(proj, gates, a_conv[e], _lane_row(a_A_log[e], A_HEADS),
                            _lane_row(a_dt_bias[e], A_HEADS), a_out_norm[e].reshape(1, A_HEAD_DIM),
                            batch=batch, seq=seq, ts=256, hb=8)
            q_rot, k_rot = rope_qk(proj, cos_t, sin_t, seq=seq, tm=512)
            y_b = dattn_heads(q_rot, k_rot, proj, b_lambda[e], b_out_norm[e].reshape(1, B_V_DIM),
                              batch=batch, seq=seq, tq=512, tk=256, lam_init=_lambda_init(layer))
            xf = ab_out_proj(y_a, y_b, ab_w_out, e, norm_post[layer], xf, tm=512)
        else:
            o = layer // 2
            proj = norm_matmul(xf, norm_pre[layer], c_w_in, tm=1024, tn=1024, w_index=o)
            xf = conv_out_proj(proj, c_conv[o], c_w_out, o, norm_post[layer], xf, seq=seq, tm=256)
        kv = norm_matmul(memf, mem_norm[layer], m_w_kv, tm=512, tn=2 * M_WIDTH, w_index=layer)
        xf = xattn_sublayer(xf, xattn_norm_pre[layer], m_w_q, kv,
                            m_w_o, layer, xattn_norm_post[layer],
                            seq=seq, n_mem=n_mem, tm=512)
    return xf.reshape(batch, seq, d)
```
